```python
import jax, jax.numpy as jnp
from jax import lax
import numpy as np


D_MODEL = 1024
BATCH = 2
SEQ = 8192
DEPTH = 2

N_MEM = 256
D_MIX = D_MODEL
CONV_HEADS = 4
CONV_DIM = D_MIX // 4
CONV_WIDTH = 3
REC_HEADS = 4
REC_DIM = D_MIX // 2
REC_HEAD_DIM = REC_DIM // REC_HEADS
REC_EXPAND = 128
REC_FDIM = REC_HEADS * REC_EXPAND
CHUNK = 64
POOL_DIM = D_MIX - CONV_DIM - REC_DIM
POOL_WINDOWS = (2, 4, 8, 16)
POOL_GROUP = POOL_DIM // len(POOL_WINDOWS)
D_IN = 3 * CONV_DIM + 2 * REC_FDIM + 2 * REC_DIM + POOL_DIM
CA_HEADS = 4
CA_HEAD_DIM = D_MODEL // CA_HEADS
D_FF = 2816
ALPHA = (2.0 * DEPTH) ** 0.25
BETA = (8.0 * DEPTH) ** -0.25
LN_EPS = 1e-5
RMS_EPS = 1e-6

kernel_name = 'hymba_style_conv_hgrn2_pool_hybrid'


def layer_norm(x, g, b):
    xf = x.astype(jnp.float32)
    mu = jnp.mean(xf, axis=-1, keepdims=True)
    var = jnp.mean(jnp.square(xf - mu), axis=-1, keepdims=True)
    return ((xf - mu) * lax.rsqrt(var + LN_EPS)).astype(x.dtype) * g + b


def swiglu(x, w_gate, w_up, w_down):
    return (jax.nn.silu(x @ w_gate) * (x @ w_up)) @ w_down


def short_gated_conv(h, b_gate, c_gate, w):
    u = c_gate * h
    s = u.shape[1]
    up = jnp.pad(u, ((0, 0), (CONV_WIDTH - 1, 0), (0, 0)))
    y = sum(w[k] * up[:, k:k + s] for k in range(CONV_WIDTH))
    return b_gate * y


def hgrn2(q, f_logit, v, g, lb, norm_g):
    bsz, s, _ = q.shape
    n_chunks = s // CHUNK
    f32 = jnp.float32
    f = lb + (1.0 - lb) * jax.nn.sigmoid(f_logit.astype(f32))
    logf = jnp.log(f)
    k = 1.0 - f

    def to_chunks(t, d):
        return t.reshape(bsz, n_chunks, CHUNK, REC_HEADS, d).transpose(1, 0, 3, 2, 4)

    qc = to_chunks(q.astype(f32), REC_EXPAND)
    kc = to_chunks(k, REC_EXPAND)
    lc = to_chunks(logf, REC_EXPAND)
    vc = to_chunks(v.astype(f32), REC_HEAD_DIM)
    mask = jnp.tril(jnp.ones((CHUNK, CHUNK), dtype=bool))

    def step(state, inp):
        qb, kb, vb, lb_ = inp
        b = jnp.cumsum(lb_, axis=2)
        diff = b[:, :, :, None, :] - b[:, :, None, :, :]
        decay = jnp.exp(jnp.where(mask[:, :, None], diff, -jnp.inf))
        attn = jnp.einsum('bhtd,bhsd,bhtsd->bhts', qb, kb, decay)
        o = (jnp.einsum('bhts,bhsv->bhtv', attn, vb)
             + jnp.einsum('bhtd,bhdv->bhtv', qb * jnp.exp(b), state))
        b_last = b[:, :, -1:, :]
        new_state = (jnp.exp(b_last[:, :, 0, :])[..., None] * state
                     + jnp.einsum('bhsd,bhsv->bhdv', kb * jnp.exp(b_last - b), vb))
        return new_state, o

    s0 = jnp.zeros((bsz, REC_HEADS, REC_EXPAND, REC_HEAD_DIM), f32)
    _, o = lax.scan(step, s0, (qc, kc, vc, lc))
    o = o.transpose(1, 0, 3, 2, 4).reshape(bsz, s, REC_HEADS, REC_HEAD_DIM)
    o = o * lax.rsqrt(jnp.mean(jnp.square(o), axis=-1, keepdims=True) + RMS_EPS)
    o = o * norm_g.reshape(REC_HEADS, REC_HEAD_DIM).astype(f32)
    o = o.reshape(bsz, s, REC_DIM) * jax.nn.sigmoid(g.astype(f32))
    return o.astype(v.dtype)


def multiscale_pool(u, w_pool, scale):
    bsz, s, _ = u.shape
    uf = u.astype(jnp.float32)
    cs = jnp.cumsum(uf, axis=1)
    t = jnp.arange(s)
    groups = []
    for gi, w in enumerate(POOL_WINDOWS):
        sl = slice(gi * POOL_GROUP, (gi + 1) * POOL_GROUP)
        c = cs[:, :, sl]
        win_sum = c - jnp.pad(c, ((0, 0), (w, 0), (0, 0)))[:, :s]
        count = jnp.minimum(t + 1, w).astype(jnp.float32)[None, :, None]
        groups.append(win_sum / count - uf[:, :, sl])
    p = jnp.stack(groups, axis=2).astype(u.dtype)
    y = jnp.einsum('bsgc,gcd->bsgd', p, w_pool).reshape(bsz, s, POOL_DIM)
    return y * scale


def cross_attn(x, mem, wq, wk, wv, wo):
    bsz, s, _ = x.shape
    m = mem.shape[1]
    q = (x @ wq).reshape(bsz, s, CA_HEADS, CA_HEAD_DIM)
    k = (mem @ wk).reshape(bsz, m, CA_HEADS, CA_HEAD_DIM)
    v = (mem @ wv).reshape(bsz, m, CA_HEADS, CA_HEAD_DIM)
    sc = jnp.einsum('bshd,bmhd->bhsm', q, k).astype(jnp.float32) * (CA_HEAD_DIM ** -0.5)
    p = jax.nn.softmax(sc, axis=-1).astype(v.dtype)
    o = jnp.einsum('bhsm,bmhd->bshd', p, v).reshape(bsz, s, D_MODEL)
    return o @ wo


def setup_inputs(seed: int = 0) -> dict:
    key = jax.random.key(seed)
    ks = jax.random.split(key, 24)
    f32 = jnp.float32
    nrm = lambda k, shp, sc: jax.random.normal(k, shp, f32) * sc
    return {
        'x': nrm(ks[0], (BATCH, SEQ, D_MODEL), 1.0),
        'mem': nrm(ks[1], (BATCH, N_MEM, D_MODEL), 1.0),
        'ffn1_gate': nrm(ks[2], (DEPTH, D_MODEL, D_FF), BETA * D_MODEL ** -0.5),
        'ffn1_up': nrm(ks[3], (DEPTH, D_MODEL, D_FF), BETA * D_MODEL ** -0.5),
        'ffn1_down': nrm(ks[4], (DEPTH, D_FF, D_MODEL), BETA * D_FF ** -0.5),
        'w_in': nrm(ks[5], (DEPTH, D_MODEL, D_IN), D_MODEL ** -0.5),
        'conv_w': nrm(ks[6], (DEPTH, CONV_WIDTH, CONV_DIM), CONV_WIDTH ** -0.5),
        'rec_lb': nrm(ks[7], (DEPTH, REC_FDIM), 0.5),
        'rec_norm_g': 1.0 + nrm(ks[8], (DEPTH, REC_DIM), 0.02),
        'pool_w': nrm(ks[9], (DEPTH, len(POOL_WINDOWS), POOL_GROUP, POOL_GROUP), POOL_GROUP ** -0.5),
        'pool_scale': 1.0 + nrm(ks[10], (DEPTH, POOL_DIM), 0.02),
        'w_out': nrm(ks[11], (DEPTH, D_MIX, D_MODEL), BETA * D_MIX ** -0.5),
        'ca_q': nrm(ks[12], (DEPTH, D_MODEL, D_MODEL), D_MODEL ** -0.5),
        'ca_k': nrm(ks[13], (DEPTH, D_MODEL, D_MODEL), D_MODEL ** -0.5),
        'ca_v': nrm(ks[14], (DEPTH, D_MODEL, D_MODEL), BETA * D_MODEL ** -0.5),
        'ca_o': nrm(ks[15], (DEPTH, D_MODEL, D_MODEL), BETA * D_MODEL ** -0.5),
        'ffn2_gate': nrm(ks[16], (DEPTH, D_MODEL, D_FF), BETA * D_MODEL ** -0.5),
        'ffn2_up': nrm(ks[17], (DEPTH, D_MODEL, D_FF), BETA * D_MODEL ** -0.5),
        'ffn2_down': nrm(ks[18], (DEPTH, D_FF, D_MODEL), BETA * D_FF ** -0.5),
        'ln_g': 1.0 + nrm(ks[19], (DEPTH, 4, D_MODEL), 0.02),
        'ln_b': nrm(ks[20], (DEPTH, 4, D_MODEL), 0.02),
    }


def reference(x, mem, ffn1_gate, ffn1_up, ffn1_down, w_in, conv_w, rec_lb, rec_norm_g,
              pool_w, pool_scale, w_out, ca_q, ca_k, ca_v, ca_o,
              ffn2_gate, ffn2_up, ffn2_down, ln_g, ln_b):
    sm = jax.nn.softmax(rec_lb.astype(jnp.float32), axis=0)
    lbs = jnp.cumsum(sm, axis=0) - sm[0:1]
    split_at = np.cumsum([CONV_DIM, CONV_DIM, CONV_DIM, REC_FDIM, REC_FDIM, REC_DIM, REC_DIM])
    h = x
    for l in range(DEPTH):
        h = layer_norm(ALPHA * h + 0.5 * swiglu(h, ffn1_gate[l], ffn1_up[l], ffn1_down[l]),
                       ln_g[l, 0], ln_b[l, 0])
        z = h @ w_in[l]
        cb, cc, ch, rq, rf, ri, rg, pu = jnp.split(z, split_at, axis=-1)
        y_conv = short_gated_conv(ch, cb, cc, conv_w[l])
        y_rec = hgrn2(rq, rf, ri, rg, lbs[l], rec_norm_g[l])
        y_pool = multiscale_pool(pu, pool_w[l], pool_scale[l])
        mix = jnp.concatenate([y_conv, y_rec, y_pool], axis=-1) @ w_out[l]
        h = layer_norm(ALPHA * h + mix, ln_g[l, 1], ln_b[l, 1])
        h = layer_norm(ALPHA * h + cross_attn(h, mem, ca_q[l], ca_k[l], ca_v[l], ca_o[l]),
                       ln_g[l, 2], ln_b[l, 2])
        h = layer_norm(ALPHA * h + 0.5 * swiglu(h, ffn2_gate[l], ffn2_up[l], ffn2_down[l]),
                       ln_g[l, 3], ln_b[l, 3])
    return h
```

```python
import functools

import jax
import jax.numpy as jnp
from jax import lax
from jax.experimental import pallas as pl
from jax.experimental.pallas import tpu as pltpu

D_MODEL = 1024
DEPTH = 2
N_MEM = 256
CONV_DIM = 256
CONV_WIDTH = 3
REC_HEADS = 4
REC_DIM = 512
REC_HEAD_DIM = 128
REC_FDIM = 512
POOL_DIM = 256
POOL_WINDOWS = (2, 4, 8, 16)
POOL_GROUP = 64
D_IN = 3 * CONV_DIM + 2 * REC_FDIM + 2 * REC_DIM + POOL_DIM
CA_HEADS = 4
CA_HEAD_DIM = 256
D_FF = 2816
ALPHA = (2.0 * DEPTH) ** 0.25
LN_EPS = 1e-5
RMS_EPS = 1e-6

F32 = jnp.float32
BF16 = jnp.bfloat16

VMEM_LIMIT_BYTES = 56 * 1024 * 1024
SUBLANES = 8
TM = 256
TS = 256
HALF = TS // 2
CONV_HIST = 8
POOL_HIST = 16
LEVELS = (1, 2, 4, 8, 16, 32, 64, 128)


def _layer_norm(y, g, b):
    mu = jnp.mean(y, axis=-1, keepdims=True)
    d = y - mu
    var = jnp.mean(d * d, axis=-1, keepdims=True)
    return d * lax.rsqrt(var + LN_EPS) * g + b


def _dot(a, b):
    return jnp.dot(a, b, preferred_element_type=F32)


def _dot_nt(a, b):
    return lax.dot_general(a, b, (((1,), (1,)), ((), ())), preferred_element_type=F32)


def _resident(shape):
    return pl.BlockSpec(shape, lambda *_: (0,) * len(shape), pipeline_mode=pl.Buffered(1))


def _ffn_kernel(h_ref, wg_ref, wu_ref, wd_ref, g_ref, b_ref, o_ref):
    x = h_ref[...]
    xb = x.astype(BF16)
    gate = _dot(xb, wg_ref[...])
    up = _dot(xb, wu_ref[...])
    act = (gate * jax.nn.sigmoid(gate) * up).astype(BF16)
    y = _dot(act, wd_ref[...])
    o_ref[...] = _layer_norm(ALPHA * x + 0.5 * y, g_ref[...], b_ref[...])


def _ffn(h, wg, wu, wd, g, b):
    t = h.shape[0]
    row = pl.BlockSpec((TM, D_MODEL), lambda i: (i, 0))
    return pl.pallas_call(
        _ffn_kernel,
        grid=(t // TM,),
        in_specs=[row, _resident(wg.shape), _resident(wu.shape), _resident(wd.shape),
                  _resident(g.shape), _resident(b.shape)],
        out_specs=row,
        out_shape=jax.ShapeDtypeStruct((t, D_MODEL), F32),
        compiler_params=pltpu.CompilerParams(
            dimension_semantics=("parallel",), vmem_limit_bytes=VMEM_LIMIT_BYTES),
        name="ffn_ln",
    )(h, wg, wu, wd, g, b)


def _mixer_kernel(h_ref, win_ref, convw_ref, reclb_ref, normg_ref, poolw_ref, poolscale_ref,
                  wout_ref, g_ref, beta_ref, o_ref,
                  state_ref, uprev_ref, pprev_ref, bcum_ref, *, layer):
    s_idx = pl.program_id(1)

    @pl.when(s_idx == 0)
    def _():
        state_ref[...] = jnp.zeros_like(state_ref)
        uprev_ref[...] = jnp.zeros_like(uprev_ref)
        pprev_ref[...] = jnp.zeros_like(pprev_ref)

    x = h_ref[...]
    xb = x.astype(BF16)

    def proj(lo, width):
        return _dot(xb, win_ref[:, lo:lo + width])

    c0 = 0
    cb = proj(c0, CONV_DIM)
    cc = proj(c0 + CONV_DIM, CONV_DIM)
    ch = proj(c0 + 2 * CONV_DIM, CONV_DIM)
    u = cc * ch
    uext = jnp.concatenate([uprev_ref[...], u], axis=0)
    u1 = pltpu.roll(uext, 1, axis=0)[CONV_HIST:]
    u2 = pltpu.roll(uext, 2, axis=0)[CONV_HIST:]
    uprev_ref[...] = u[TS - CONV_HIST:]
    cw = convw_ref[...]
    y_conv = cb * (cw[2:3] * u + cw[1:2] * u1 + cw[0:1] * u2)

    p0 = 3 * CONV_DIM + 2 * REC_FDIM + 2 * REC_DIM
    pu = proj(p0, POOL_DIM)
    pext = jnp.concatenate([pprev_ref[...], pu], axis=0)
    pprev_ref[...] = pu[TS - POOL_HIST:]
    lane = lax.broadcasted_iota(jnp.int32, (TS, POOL_DIM), 1)
    tpos = s_idx * TS + lax.broadcasted_iota(jnp.int32, (TS, POOL_DIM), 0)
    acc = pext
    wsum = None
    win = None
    span = 1
    for gi, w in enumerate(POOL_WINDOWS):
        while span < w:
            acc = acc + pltpu.roll(acc, span, axis=0)
            span *= 2
        part = acc[POOL_HIST:]
        in_group = lane >= gi * POOL_GROUP
        wsum = part if wsum is None else jnp.where(in_group, part, wsum)
        win = jnp.full((TS, POOL_DIM), w, jnp.int32) if win is None else jnp.where(in_group, w, win)
    count = jnp.minimum(tpos + 1, win).astype(F32)
    pooled = (wsum / count - pu).astype(BF16)
    y_pool = _dot(pooled, poolw_ref[...]) * poolscale_ref[...]

    r0 = 3 * CONV_DIM
    rq = proj(r0, REC_FDIM)
    rf = proj(r0 + REC_FDIM, REC_FDIM)
    rv = proj(r0 + 2 * REC_FDIM, REC_DIM)
    rg = proj(r0 + 2 * REC_FDIM + REC_DIM, REC_DIM)

    rl = reclb_ref[...]
    e = jnp.exp(rl - jnp.max(rl, axis=0, keepdims=True))
    sm = e / jnp.sum(e, axis=0, keepdims=True)
    lb = jnp.sum(sm[0:layer + 1], axis=0, keepdims=True) - sm[0:1]

    f = lb + (1.0 - lb) * jax.nn.sigmoid(rf)
    logf = jnp.log(f)
    k = 1.0 - f

    row2 = lax.broadcasted_iota(jnp.int32, (TS, TS), 0)
    col2 = lax.broadcasted_iota(jnp.int32, (TS, TS), 1)
    tri = jnp.where(row2 >= col2, 1.0, 0.0).astype(BF16)
    l_hi = logf.astype(BF16)
    rem = logf - l_hi.astype(F32)
    l_mid = rem.astype(BF16)
    l_lo = (rem - l_mid.astype(F32)).astype(BF16)
    bcum = _dot(tri, l_hi) + _dot(tri, l_mid) + _dot(tri, l_lo)
    bcum_ref[...] = bcum
    b_last = bcum_ref[TS - 1:TS, :]

    q_in = (rq * jnp.exp(bcum)).astype(BF16)
    k_out = (k * jnp.exp(b_last - bcum)).astype(BF16)
    decay_all = jnp.exp(b_last)

    rbit = lax.broadcasted_iota(jnp.int32, (TS, REC_FDIM), 0)
    b3 = bcum.reshape(TS // SUBLANES, SUBLANES, REC_FDIM)
    sub3 = lax.broadcasted_iota(jnp.int32, b3.shape, 1)

    def ref_rows(half):
        if half == 1:
            r = jnp.where((sub3 & 1) == 1, pltpu.roll(b3, 1, axis=1), b3)
            return r.reshape(TS, REC_FDIM)
        if half == 2:
            m = sub3 & 3
            r = jnp.where(m == 0, pltpu.roll(b3, SUBLANES - 1, axis=1),
                          jnp.where(m == 1, b3,
                                    jnp.where(m == 2, pltpu.roll(b3, 1, axis=1),
                                              pltpu.roll(b3, 2, axis=1))))
            return r.reshape(TS, REC_FDIM)
        pieces = [
            jnp.broadcast_to(bcum_ref[pl.ds(2 * half * gidx + half - 1, 1), :], (2 * half, REC_FDIM))
            for gidx in range(TS // (2 * half))
        ]
        return pieces[0] if len(pieces) == 1 else jnp.concatenate(pieces, axis=0)

    w_lvl = {}
    for half in LEVELS:
        gfac = jnp.exp(-jnp.abs(bcum - ref_rows(half)))
        w_lvl[half] = (jnp.where((rbit & half) != 0, rq, k) * gfac).astype(BF16)

    trow = lax.broadcasted_iota(jnp.int32, (HALF, HALF), 0)
    tcol = lax.broadcasted_iota(jnp.int32, (HALF, HALF), 1)
    xor = trow ^ tcol
    causal = trow > tcol
    zero_blk = jnp.zeros((HALF, HALF), F32)

    sg = jax.nn.sigmoid(rg)
    ng = normg_ref[...]
    y_rec = []
    for hd in range(REC_HEADS):
        sl = slice(hd * REC_HEAD_DIM, (hd + 1) * REC_HEAD_DIM)
        qh = rq[:, sl]
        kh = k[:, sl]
        vh = rv[:, sl]
        vb = vh.astype(BF16)
        diag = []
        for d in range(2):
            rows = slice(d * HALF, (d + 1) * HALF)
            blk = zero_blk
            for half in LEVELS[:-1]:
                wl = w_lvl[half][rows, sl]
                blk = jnp.where(xor >= half, _dot_nt(wl, wl), blk)
            diag.append(jnp.where(causal, blk, 0.0))
        wl = w_lvl[HALF]
        cross = _dot_nt(wl[HALF:, sl], wl[:HALF, sl])
        attn = jnp.concatenate(
            [jnp.concatenate([diag[0], zero_blk], axis=1),
             jnp.concatenate([cross, diag[1]], axis=1)], axis=0).astype(BF16)
        st = state_ref[hd]
        o = (_dot(attn, vb)
             + jnp.sum(qh * kh, axis=-1, keepdims=True) * vh
             + _dot_nt(q_in[:, sl], st.astype(BF16)))
        state_ref[hd] = st * decay_all[:, sl] + _dot(vh.T.astype(BF16), k_out[:, sl])
        o = o * lax.rsqrt(jnp.mean(o * o, axis=-1, keepdims=True) + RMS_EPS)
        y_rec.append(o * ng[:, sl] * sg[:, sl])

    mix = jnp.concatenate([y_conv] + y_rec + [y_pool], axis=-1).astype(BF16)
    y = _dot(mix, wout_ref[...])
    o_ref[...] = _layer_norm(ALPHA * x + y, g_ref[...], beta_ref[...])


def _mixer(h, batch, layer, w_in, conv_w, rec_lb, norm_g, pool_w, pool_scale, w_out, g, b):
    t = h.shape[0]
    steps = t // batch // TS
    row = pl.BlockSpec((TS, D_MODEL), lambda bi, si: (bi * steps + si, 0))
    params = (w_in, conv_w, rec_lb, norm_g, pool_w, pool_scale, w_out, g, b)
    return pl.pallas_call(
        functools.partial(_mixer_kernel, layer=layer),
        grid=(batch, steps),
        in_specs=[row] + [_resident(p.shape) for p in params],
        out_specs=row,
        out_shape=jax.ShapeDtypeStruct((t, D_MODEL), F32),
        scratch_shapes=[
            pltpu.VMEM((REC_HEADS, REC_HEAD_DIM, REC_HEAD_DIM), F32),
            pltpu.VMEM((CONV_HIST, CONV_DIM), F32),
            pltpu.VMEM((POOL_HIST, POOL_DIM), F32),
            pltpu.VMEM((TS, REC_FDIM), F32),
        ],
        compiler_params=pltpu.CompilerParams(
            dimension_semantics=("parallel", "arbitrary"), vmem_limit_bytes=VMEM_LIMIT_BYTES),
        name="mixer_ln",
    )(h, *params)


def _kv_kernel(mem_ref, wk_ref, wv_ref, k_ref, v_ref):
    mb = mem_ref[...].astype(BF16)
    k_ref[...] = _dot(mb, wk_ref[...]).astype(BF16)
    v_ref[...] = _dot(mb, wv_ref[...]).astype(BF16)


def _kv(mem, wk, wv):
    m = mem.shape[0]
    nblk = 512
    wspec = pl.BlockSpec((D_MODEL, nblk), lambda j: (0, j))
    ospec = pl.BlockSpec((m, nblk), lambda j: (0, j))
    return pl.pallas_call(
        _kv_kernel,
        grid=(D_MODEL // nblk,),
        in_specs=[_resident(mem.shape), wspec, wspec],
        out_specs=[ospec, ospec],
        out_shape=[jax.ShapeDtypeStruct((m, D_MODEL), BF16)] * 2,
        compiler_params=pltpu.CompilerParams(
            dimension_semantics=("parallel",), vmem_limit_bytes=VMEM_LIMIT_BYTES),
        name="mem_kv",
    )(mem, wk, wv)


def _ca_kernel(h_ref, k_ref, v_ref, wq_ref, wo_ref, g_ref, b_ref, o_ref):
    x = h_ref[...]
    q = _dot(x.astype(BF16), wq_ref[...])
    heads = []
    for hd in range(CA_HEADS):
        sl = slice(hd * CA_HEAD_DIM, (hd + 1) * CA_HEAD_DIM)
        s = _dot_nt(q[:, sl].astype(BF16), k_ref[:, sl]) * (CA_HEAD_DIM ** -0.5)
        p = jnp.exp(s - jnp.max(s, axis=-1, keepdims=True))
        p = p / jnp.sum(p, axis=-1, keepdims=True)
        heads.append(_dot(p.astype(BF16), v_ref[:, sl]))
    o = jnp.concatenate(heads, axis=-1).astype(BF16)
    y = _dot(o, wo_ref[...])
    o_ref[...] = _layer_norm(ALPHA * x + y, g_ref[...], b_ref[...])


def _cross_attn(h, batch, kmem, vmem, wq, wo, g, b):
    t = h.shape[0]
    per_batch = t // batch // TM
    row = pl.BlockSpec((TM, D_MODEL), lambda i: (i, 0))
    mem = pl.BlockSpec((N_MEM, D_MODEL), lambda i: (i // per_batch, 0))
    return pl.pallas_call(
        _ca_kernel,
        grid=(t // TM,),
        in_specs=[row, mem, mem, _resident(wq.shape), _resident(wo.shape),
                  _resident(g.shape), _resident(b.shape)],
        out_specs=row,
        out_shape=jax.ShapeDtypeStruct((t, D_MODEL), F32),
        compiler_params=pltpu.CompilerParams(
            dimension_semantics=("parallel",), vmem_limit_bytes=VMEM_LIMIT_BYTES),
        name="cross_attn_ln",
    )(h, kmem, vmem, wq, wo, g, b)


def _pool_block_diag(w):
    ng, c, d = w.shape
    out = jnp.zeros((ng * c, ng * d), w.dtype)
    for gi in range(ng):
        out = out.at[gi * c:(gi + 1) * c, gi * d:(gi + 1) * d].set(w[gi])
    return out


def kernel(x, mem, ffn1_gate, ffn1_up, ffn1_down, w_in, conv_w, rec_lb, rec_norm_g, pool_w,
           pool_scale, w_out, ca_q, ca_k, ca_v, ca_o, ffn2_gate, ffn2_up, ffn2_down, ln_g, ln_b):
    batch, seq, _ = x.shape
    h = x.reshape(batch * seq, D_MODEL)
    mem2 = mem.reshape(batch * N_MEM, D_MODEL)
    bf = lambda a: a.astype(BF16)
    vec = lambda a: a.reshape(1, -1)
    for l in range(DEPTH):
        h = _ffn(h, bf(ffn1_gate[l]), bf(ffn1_up[l]), bf(ffn1_down[l]),
                 vec(ln_g[l, 0]), vec(ln_b[l, 0]))
        h = _mixer(h, batch, l, bf(w_in[l]), conv_w[l], rec_lb, vec(rec_norm_g[l]),
                   bf(_pool_block_diag(pool_w[l])), vec(pool_scale[l]), bf(w_out[l]),
                   vec(ln_g[l, 1]), vec(ln_b[l, 1]))
        kmem, vmem = _kv(mem2, bf(ca_k[l]), bf(ca_v[l]))
        h = _cross_attn(h, batch, kmem, vmem, bf(ca_q[l]), bf(ca_o[l]),
                        vec(ln_g[l, 2]), vec(ln_b[l, 2]))
        h = _ffn(h, bf(ffn2_gate[l]), bf(ffn2_up[l]), bf(ffn2_down[l]),
                 vec(ln_g[l, 3]), vec(ln_b[l, 3]))
    return h.reshape(batch, seq, D_MODEL)
```

```python
import functools

import jax
import jax.numpy as jnp
from jax import lax
from jax.experimental import pallas as pl
from jax.experimental.pallas import tpu as pltpu

D_MODEL = 1024
DEPTH = 2
N_MEM = 256
CONV_DIM = 256
CONV_WIDTH = 3
REC_HEADS = 4
REC_DIM = 512
REC_HEAD_DIM = 128
REC_FDIM = 512
POOL_DIM = 256
POOL_WINDOWS = (2, 4, 8, 16)
POOL_GROUP = 64
D_IN = 3 * CONV_DIM + 2 * REC_FDIM + 2 * REC_DIM + POOL_DIM
CA_HEADS = 4
CA_HEAD_DIM = 256
D_FF = 2816
ALPHA = (2.0 * DEPTH) ** 0.25
LN_EPS = 1e-5
RMS_EPS = 1e-6

F32 = jnp.float32
BF16 = jnp.bfloat16

VMEM_LIMIT_BYTES = 56 * 1024 * 1024
SUBLANES = 8
TM = 512
TS = 256
HALF = TS // 2
CONV_HIST = 8
POOL_HIST = 16
LEVELS = (1, 2, 4, 8, 16, 32, 64, 128)
FILL_COLS = 256
FILL_PLAN_FIRST = (2, 1) + (1, 1, 1, 1, 1, 0, 0, 0) + (1, 1, 1, 1)
FILL_PLAN_SECOND = (0, 0) + (1, 1, 1, 1, 1, 1, 1, 1) + (1, 1, 1, 1)


def _layer_norm(y, g, b):
    mu = jnp.mean(y, axis=-1, keepdims=True)
    d = y - mu
    var = jnp.mean(d * d, axis=-1, keepdims=True)
    return d * lax.rsqrt(var + LN_EPS) * g + b


def _dot(a, b):
    return jnp.dot(a, b, preferred_element_type=F32)


def _dot_nt(a, b):
    return lax.dot_general(a, b, (((1,), (1,)), ((), ())), preferred_element_type=F32)


def _resident(shape):
    return pl.BlockSpec(shape, lambda *_: (0,) * len(shape), pipeline_mode=pl.Buffered(1))


def _ffn_kernel(h_ref, wg_ref, wu_ref, wd_ref, g_ref, b_ref, o_ref):
    x = h_ref[...]
    xb = x.astype(BF16)
    gate = _dot(xb, wg_ref[...])
    up = _dot(xb, wu_ref[...])
    act = (gate * jax.nn.sigmoid(gate) * up).astype(BF16)
    y = _dot(act, wd_ref[...])
    o_ref[...] = _layer_norm(ALPHA * x + 0.5 * y, g_ref[...], b_ref[...])


def _ffn(h, wg, wu, wd, g, b):
    t = h.shape[0]
    row = pl.BlockSpec((TM, D_MODEL), lambda i: (i, 0))
    return pl.pallas_call(
        _ffn_kernel,
        grid=(t // TM,),
        in_specs=[row, _resident(wg.shape), _resident(wu.shape), _resident(wd.shape),
                  _resident(g.shape), _resident(b.shape)],
        out_specs=row,
        out_shape=jax.ShapeDtypeStruct((t, D_MODEL), F32),
        compiler_params=pltpu.CompilerParams(
            dimension_semantics=("parallel",), vmem_limit_bytes=VMEM_LIMIT_BYTES),
        name="ffn_ln",
    )(h, wg, wu, wd, g, b)


def _mixer_tile(x, z, tile_idx, layer, fill, convw_ref, reclb_ref, normg_ref, poolw_ref,
                poolscale_ref, wout_ref, g_ref, beta_ref, state_ref, uprev_ref, pprev_ref, bcum_ref):
    def proj(lo, width):
        return z[:, lo:lo + width]

    fill()

    c0 = 0
    cb = proj(c0, CONV_DIM)
    cc = proj(c0 + CONV_DIM, CONV_DIM)
    ch = proj(c0 + 2 * CONV_DIM, CONV_DIM)
    u = cc * ch
    uext = jnp.concatenate([uprev_ref[...], u], axis=0)
    u1 = pltpu.roll(uext, 1, axis=0)[CONV_HIST:]
    u2 = pltpu.roll(uext, 2, axis=0)[CONV_HIST:]
    uprev_ref[...] = u[TS - CONV_HIST:]
    cw = convw_ref[...]
    y_conv = cb * (cw[2:3] * u + cw[1:2] * u1 + cw[0:1] * u2)

    p0 = 3 * CONV_DIM + 2 * REC_FDIM + 2 * REC_DIM
    pu = proj(p0, POOL_DIM)
    pext = jnp.concatenate([pprev_ref[...], pu], axis=0)
    pprev_ref[...] = pu[TS - POOL_HIST:]
    lane = lax.broadcasted_iota(jnp.int32, (TS, POOL_DIM), 1)
    tpos = tile_idx * TS + lax.broadcasted_iota(jnp.int32, (TS, POOL_DIM), 0)
    acc = pext
    wsum = None
    win = None
    span = 1
    for gi, w in enumerate(POOL_WINDOWS):
        while span < w:
            acc = acc + pltpu.roll(acc, span, axis=0)
            span *= 2
        part = acc[POOL_HIST:]
        in_group = lane >= gi * POOL_GROUP
        wsum = part if wsum is None else jnp.where(in_group, part, wsum)
        win = jnp.full((TS, POOL_DIM), w, jnp.int32) if win is None else jnp.where(in_group, w, win)
    count = jnp.minimum(tpos + 1, win).astype(F32)
    pooled = (wsum / count - pu).astype(BF16)
    y_pool = _dot(pooled, poolw_ref[...]) * poolscale_ref[...]
    fill()

    r0 = 3 * CONV_DIM
    rq = proj(r0, REC_FDIM)
    rf = proj(r0 + REC_FDIM, REC_FDIM)
    rv = proj(r0 + 2 * REC_FDIM, REC_DIM)
    rg = proj(r0 + 2 * REC_FDIM + REC_DIM, REC_DIM)

    rl = reclb_ref[...]
    e = jnp.exp(rl - jnp.max(rl, axis=0, keepdims=True))
    sm = e / jnp.sum(e, axis=0, keepdims=True)
    lb = jnp.sum(sm[0:layer + 1], axis=0, keepdims=True) - sm[0:1]

    f = lb + (1.0 - lb) * jax.nn.sigmoid(rf)
    log2f = jnp.log2(f)
    k = 1.0 - f

    row2 = lax.broadcasted_iota(jnp.int32, (TS, TS), 0)
    col2 = lax.broadcasted_iota(jnp.int32, (TS, TS), 1)
    tri = jnp.where(row2 >= col2, 1.0, 0.0).astype(BF16)
    l_hi = log2f.astype(BF16)
    l_lo = (log2f - l_hi.astype(F32)).astype(BF16)
    bcum = _dot(tri, l_hi) + _dot(tri, l_lo)
    bcum_ref[...] = bcum
    b_last = bcum_ref[TS - 1:TS, :]

    q_in = (rq * jnp.exp2(bcum)).astype(BF16)
    k_out = (k * jnp.exp2(b_last - bcum)).astype(BF16)
    decay_all = jnp.exp2(b_last)

    rbit = lax.broadcasted_iota(jnp.int32, (TS, REC_FDIM), 0)
    b3 = bcum.reshape(TS // SUBLANES, SUBLANES, REC_FDIM)
    sub3 = lax.broadcasted_iota(jnp.int32, b3.shape, 1)

    def level_operand(half):
        if half >= SUBLANES:
            parts = []
            for gidx in range(TS // (2 * half)):
                lo = 2 * half * gidx
                ref = bcum_ref[lo + half - 1:lo + half, :]
                parts.append(k[lo:lo + half] * jnp.exp2(ref - bcum[lo:lo + half]))
                parts.append(rq[lo + half:lo + 2 * half]
                             * jnp.exp2(bcum[lo + half:lo + 2 * half] - ref))
            return jnp.concatenate(parts, axis=0).astype(BF16)
        if half == 1:
            ref = jnp.where((sub3 & 1) == 1, pltpu.roll(b3, 1, axis=1), b3)
        elif half == 2:
            m = sub3 & 3
            ref = jnp.where(m == 0, pltpu.roll(b3, SUBLANES - 1, axis=1),
                            jnp.where(m == 1, b3,
                                      jnp.where(m == 2, pltpu.roll(b3, 1, axis=1),
                                                pltpu.roll(b3, 2, axis=1))))
        else:
            ref = jnp.concatenate(
                [jnp.broadcast_to(bcum_ref[SUBLANES * gidx + half - 1:SUBLANES * gidx + half, :],
                                  (1, SUBLANES, REC_FDIM))
                 for gidx in range(TS // SUBLANES)], axis=0)
        gfac = jnp.exp2(-jnp.abs(b3 - ref)).reshape(TS, REC_FDIM)
        return (jnp.where((rbit & half) != 0, rq, k) * gfac).astype(BF16)

    w_lvl = {}
    for half in LEVELS:
        fill()
        w_lvl[half] = level_operand(half)

    trow = lax.broadcasted_iota(jnp.int32, (HALF, HALF), 0)
    tcol = lax.broadcasted_iota(jnp.int32, (HALF, HALF), 1)
    xor = trow ^ tcol
    causal = trow > tcol
    lvl_mask = {half: xor >= half for half in LEVELS[:-1]}
    zero_blk = jnp.zeros((HALF, HALF), F32)

    sg = jax.nn.sigmoid(rg)
    ng = normg_ref[...]
    y_rec = []
    for hd in range(REC_HEADS):
        sl = slice(hd * REC_HEAD_DIM, (hd + 1) * REC_HEAD_DIM)
        qh = rq[:, sl]
        kh = k[:, sl]
        vh = rv[:, sl]
        vb = vh.astype(BF16)
        diag = []
        for d in range(2):
            rows = slice(d * HALF, (d + 1) * HALF)
            blk = zero_blk
            for half in LEVELS[:-1]:
                wl = w_lvl[half][rows, sl]
                blk = jnp.where(lvl_mask[half], _dot_nt(wl, wl), blk)
            diag.append(jnp.where(causal, blk, 0.0))
        wl = w_lvl[HALF]
        cross = _dot_nt(wl[HALF:, sl], wl[:HALF, sl])
        fill()
        attn = jnp.concatenate(
            [jnp.concatenate([diag[0], zero_blk], axis=1),
             jnp.concatenate([cross, diag[1]], axis=1)], axis=0).astype(BF16)
        st = state_ref[hd]
        o = (_dot(attn, vb)
             + jnp.sum(qh * kh, axis=-1, keepdims=True) * vh
             + _dot_nt(q_in[:, sl], st.astype(BF16)))
        state_ref[hd] = st * decay_all[:, sl] + _dot(vh.T.astype(BF16), k_out[:, sl])
        o = o * lax.rsqrt(jnp.mean(o * o, axis=-1, keepdims=True) + RMS_EPS)
        y_rec.append(o * ng[:, sl] * sg[:, sl])

    mix = jnp.concatenate([y_conv] + y_rec + [y_pool], axis=-1).astype(BF16)
    y = _dot(mix, wout_ref[...])
    return _layer_norm(ALPHA * x + y, g_ref[...], beta_ref[...])


def _mixer_kernel(h_ref, hnext_ref, win_ref, convw_ref, reclb_ref, normg_ref, poolw_ref,
                  poolscale_ref, wout_ref, g_ref, beta_ref, o_ref,
                  z0_ref, z1_ref, state_ref, uprev_ref, pprev_ref, bcum_ref, *, layer):
    s_idx = pl.program_id(1)

    @pl.when(s_idx == 0)
    def _():
        state_ref[...] = jnp.zeros_like(state_ref)
        uprev_ref[...] = jnp.zeros_like(uprev_ref)
        pprev_ref[...] = jnp.zeros_like(pprev_ref)

    @pl.when((pl.program_id(0) == 0) & (s_idx == 0))
    def _():
        z0_ref[...] = _dot(h_ref[0:TS].astype(BF16), win_ref[...])

    tile_refs = (convw_ref, reclb_ref, normg_ref, poolw_ref, poolscale_ref, wout_ref, g_ref,
                 beta_ref, state_ref, uprev_ref, pprev_ref, bcum_ref)

    def projector(xb, dst_ref, plan):
        todo = list(range(0, D_IN, FILL_COLS))
        counts = list(plan)

        def fill(flush=False):
            for _ in range(len(todo) if flush else counts.pop(0)):
                lo = todo.pop(0)
                dst_ref[:, lo:lo + FILL_COLS] = _dot(xb, win_ref[:, lo:lo + FILL_COLS])

        return fill

    fill = projector(h_ref[TS:2 * TS].astype(BF16), z1_ref, FILL_PLAN_FIRST)
    o_ref[0:TS] = _mixer_tile(h_ref[0:TS], z0_ref[...], 2 * s_idx, layer, fill, *tile_refs)
    fill(flush=True)
    fill = projector(hnext_ref[...].astype(BF16), z0_ref, FILL_PLAN_SECOND)
    o_ref[TS:2 * TS] = _mixer_tile(h_ref[TS:2 * TS], z1_ref[...], 2 * s_idx + 1, layer, fill,
                                   *tile_refs)
    fill(flush=True)


def _mixer(h, batch, layer, w_in, conv_w, rec_lb, norm_g, pool_w, pool_scale, w_out, g, b):
    t = h.shape[0]
    steps = t // batch // (2 * TS)
    last_tile = t // TS - 1
    row = pl.BlockSpec((2 * TS, D_MODEL), lambda bi, si: (bi * steps + si, 0))
    nxt = pl.BlockSpec((TS, D_MODEL),
                       lambda bi, si: (jnp.minimum(2 * (bi * steps + si) + 2, last_tile), 0))
    params = (w_in, conv_w, rec_lb, norm_g, pool_w, pool_scale, w_out, g, b)
    return pl.pallas_call(
        functools.partial(_mixer_kernel, layer=layer),
        grid=(batch, steps),
        in_specs=[row, nxt] + [_resident(p.shape) for p in params],
        out_specs=row,
        out_shape=jax.ShapeDtypeStruct((t, D_MODEL), F32),
        scratch_shapes=[
            pltpu.VMEM((TS, D_IN), F32),
            pltpu.VMEM((TS, D_IN), F32),
            pltpu.VMEM((REC_HEADS, REC_HEAD_DIM, REC_HEAD_DIM), F32),
            pltpu.VMEM((CONV_HIST, CONV_DIM), F32),
            pltpu.VMEM((POOL_HIST, POOL_DIM), F32),
            pltpu.VMEM((TS, REC_FDIM), F32),
        ],
        compiler_params=pltpu.CompilerParams(
            dimension_semantics=("arbitrary", "arbitrary"), vmem_limit_bytes=VMEM_LIMIT_BYTES),
        name="mixer_ln",
    )(h, h, *params)


def _kv_kernel(mem_ref, wk_ref, wv_ref, k_ref, v_ref):
    mb = mem_ref[...].astype(BF16)
    k_ref[...] = _dot(mb, wk_ref[...]).astype(BF16)
    v_ref[...] = _dot(mb, wv_ref[...]).astype(BF16)


def _kv(mem, wk, wv):
    m = mem.shape[0]
    nblk = 512
    wspec = pl.BlockSpec((D_MODEL, nblk), lambda j: (0, j))
    ospec = pl.BlockSpec((m, nblk), lambda j: (0, j))
    return pl.pallas_call(
        _kv_kernel,
        grid=(D_MODEL // nblk,),
        in_specs=[_resident(mem.shape), wspec, wspec],
        out_specs=[ospec, ospec],
        out_shape=[jax.ShapeDtypeStruct((m, D_MODEL), BF16)] * 2,
        compiler_params=pltpu.CompilerParams(
            dimension_semantics=("parallel",), vmem_limit_bytes=VMEM_LIMIT_BYTES),
        name="mem_kv",
    )(mem, wk, wv)


def _ca_kernel(h_ref, k_ref, v_ref, wq_ref, wo_ref, g_ref, b_ref, o_ref):
    x = h_ref[...]
    q = _dot(x.astype(BF16), wq_ref[...])
    heads = []
    for hd in range(CA_HEADS):
        sl = slice(hd * CA_HEAD_DIM, (hd + 1) * CA_HEAD_DIM)
        s = _dot_nt(q[:, sl].astype(BF16), k_ref[:, sl]) * (CA_HEAD_DIM ** -0.5)
        p = jnp.exp(s - jnp.max(s, axis=-1, keepdims=True))
        p = p / jnp.sum(p, axis=-1, keepdims=True)
        heads.append(_dot(p.astype(BF16), v_ref[:, sl]))
    o = jnp.concatenate(heads, axis=-1).astype(BF16)
    y = _dot(o, wo_ref[...])
    o_ref[...] = _layer_norm(ALPHA * x + y, g_ref[...], b_ref[...])


def _cross_attn(h, batch, kmem, vmem, wq, wo, g, b):
    t = h.shape[0]
    per_batch = t // batch // TM
    row = pl.BlockSpec((TM, D_MODEL), lambda i: (i, 0))
    mem = pl.BlockSpec((N_MEM, D_MODEL), lambda i: (i // per_batch, 0))
    return pl.pallas_call(
        _ca_kernel,
        grid=(t // TM,),
        in_specs=[row, mem, mem, _resident(wq.shape), _resident(wo.shape),
                  _resident(g.shape), _resident(b.shape)],
        out_specs=row,
        out_shape=jax.ShapeDtypeStruct((t, D_MODEL), F32),
        compiler_params=pltpu.CompilerParams(
            dimension_semantics=("parallel",), vmem_limit_bytes=VMEM_LIMIT_BYTES),
        name="cross_attn_ln",
    )(h, kmem, vmem, wq, wo, g, b)


def _pool_block_diag(w):
    ng, c, d = w.shape
    out = jnp.zeros((ng * c, ng * d), w.dtype)
    for gi in range(ng):
        out = out.at[gi * c:(gi + 1) * c, gi * d:(gi + 1) * d].set(w[gi])
    return out


def kernel(x, mem, ffn1_gate, ffn1_up, ffn1_down, w_in, conv_w, rec_lb, rec_norm_g, pool_w,
           pool_scale, w_out, ca_q, ca_k, ca_v, ca_o, ffn2_gate, ffn2_up, ffn2_down, ln_g, ln_b):
    batch, seq, _ = x.shape
    h = x.reshape(batch * seq, D_MODEL)
    mem2 = mem.reshape(batch * N_MEM, D_MODEL)
    bf = lambda a: a.astype(BF16)
    vec = lambda a: a.reshape(1, -1)
    for l in range(DEPTH):
        h = _ffn(h, bf(ffn1_gate[l]), bf(ffn1_up[l]), bf(ffn1_down[l]),
                 vec(ln_g[l, 0]), vec(ln_b[l, 0]))
        h = _mixer(h, batch, l, bf(w_in[l]), conv_w[l], rec_lb, vec(rec_norm_g[l]),
                   bf(_pool_block_diag(pool_w[l])), vec(pool_scale[l]), bf(w_out[l]),
                   vec(ln_g[l, 1]), vec(ln_b[l, 1]))
        kmem, vmem = _kv(mem2, bf(ca_k[l]), bf(ca_v[l]))
        h = _cross_attn(h, batch, kmem, vmem, bf(ca_q[l]), bf(ca_o[l]),
                        vec(ln_g[l, 2]), vec(ln_b[l, 2]))
        h = _ffn(h, bf(ffn2_gate[l]), bf(ffn2_up[l]), bf(ffn2_down[l]),
                 vec(ln_g[l, 3]), vec(ln_b[l, 3]))
    return h.reshape(batch, seq, D_MODEL)
```

```python
import functools

import jax
import jax.numpy as jnp
from jax import lax
from jax.experimental import pallas as pl
from jax.experimental.pallas import tpu as pltpu

D_MODEL = 1024
DEPTH = 2
N_MEM = 256
CONV_DIM = 256
CONV_WIDTH = 3
REC_HEADS = 4
REC_DIM = 512
REC_HEAD_DIM = 128
REC_FDIM = 512
POOL_DIM = 256
POOL_WINDOWS = (2, 4, 8, 16)
POOL_GROUP = 64
D_IN = 3 * CONV_DIM + 2 * REC_FDIM + 2 * REC_DIM + POOL_DIM
CA_HEADS = 4
CA_HEAD_DIM = 256
D_FF = 2816
ALPHA = (2.0 * DEPTH) ** 0.25
LN_EPS = 1e-5
RMS_EPS = 1e-6

F32 = jnp.float32
BF16 = jnp.bfloat16

VMEM_LIMIT_BYTES = 56 * 1024 * 1024
SUBLANES = 8
TM = 1024
SUB = 256
WCHUNKS = 8
KV_COLS = 512
TS = 256
HALF = TS // 2
CONV_HIST = 8
POOL_HIST = 16
LEVELS = (1, 2, 4, 8, 16, 32, 64, 128)
FILL_COLS = 256
FILL_PLAN_FIRST = (2, 1) + (1, 1, 1, 1, 1, 0, 0, 0) + (1, 1, 1, 1)
FILL_PLAN_SECOND = (0, 0) + (1, 1, 1, 1, 1, 1, 1, 1) + (1, 1, 1, 1)


def _layer_norm(y, g, b):
    mu = jnp.mean(y, axis=-1, keepdims=True)
    d = y - mu
    var = jnp.mean(d * d, axis=-1, keepdims=True)
    return d * lax.rsqrt(var + LN_EPS) * g + b


def _dot(a, b):
    return jnp.dot(a, b, preferred_element_type=F32)


def _dot_nt(a, b):
    return lax.dot_general(a, b, (((1,), (1,)), ((), ())), preferred_element_type=F32)


def _resident(shape):
    return pl.BlockSpec(shape, lambda *_: (0,) * len(shape), pipeline_mode=pl.Buffered(1))


def _staged(stacked_shape, layer):
    _, rows, cols = stacked_shape
    return pl.BlockSpec((None, rows // WCHUNKS, cols),
                        lambda i: (layer, jnp.minimum(i, WCHUNKS - 1), 0))


def _ln_params(g_ref, b_ref, layer, idx):
    return g_ref[layer, idx:idx + 1, :], b_ref[layer, idx:idx + 1, :]


def _token_rows(rows):
    return pl.BlockSpec((rows, D_MODEL), lambda i: (jnp.maximum(i - WCHUNKS, 0), 0))


def _stage_chunk(i, src_ref, dst_ref, scale=None):
    rows = src_ref.shape[0]
    w = src_ref[...] if scale is None else src_ref[...] * scale
    dst_ref[pl.ds(pl.multiple_of(i * rows, rows), rows), :] = w.astype(BF16)


def _ffn_kernel(h_ref, wg_ref, wu_ref, wd_ref, g_ref, b_ref, o_ref, wg_s, wu_s, wd_s,
                *, layer, ln_idx):
    step = pl.program_id(0)
    ln_g, ln_b = _ln_params(g_ref, b_ref, layer, ln_idx)

    @pl.when(step < WCHUNKS)
    def _():
        _stage_chunk(step, wg_ref, wg_s)
        _stage_chunk(step, wu_ref, wu_s)
        _stage_chunk(step, wd_ref, wd_s, scale=0.5)

    def up_stage(i):
        xb = h_ref[i * SUB:(i + 1) * SUB].astype(BF16)
        gate = _dot(xb, wg_s[...])
        up = _dot(xb, wu_s[...])
        return (gate * jax.nn.sigmoid(gate) * up).astype(BF16)

    def down_stage(i, act):
        rows = slice(i * SUB, (i + 1) * SUB)
        y = _dot(act, wd_s[...])
        o_ref[rows] = _layer_norm(ALPHA * h_ref[rows] + y, ln_g, ln_b)

    @pl.when(step >= WCHUNKS)
    def _():
        act = up_stage(0)
        for i in range(1, TM // SUB):
            nxt = up_stage(i)
            down_stage(i - 1, act)
            act = nxt
        down_stage(TM // SUB - 1, act)


def _ffn(h, layer, ln_idx, wg, wu, wd, g, b):
    t = h.shape[0]
    return pl.pallas_call(
        functools.partial(_ffn_kernel, layer=layer, ln_idx=ln_idx),
        grid=(WCHUNKS + t // TM,),
        in_specs=[_token_rows(TM), _staged(wg.shape, layer), _staged(wu.shape, layer),
                  _staged(wd.shape, layer), _resident(g.shape), _resident(b.shape)],
        out_specs=_token_rows(TM),
        out_shape=jax.ShapeDtypeStruct((t, D_MODEL), F32),
        scratch_shapes=[pltpu.VMEM(wg.shape[1:], BF16), pltpu.VMEM(wu.shape[1:], BF16),
                        pltpu.VMEM(wd.shape[1:], BF16)],
        compiler_params=pltpu.CompilerParams(
            dimension_semantics=("arbitrary",), vmem_limit_bytes=VMEM_LIMIT_BYTES),
        name="ffn_ln",
    )(h, wg, wu, wd, g, b)


def _mixer_tile(x, z, tile_idx, layer, fill, convw_ref, reclb_ref, normg_ref, poolw_ref,
                poolscale_ref, wout_ref, g_ref, beta_ref, state_ref, uprev_ref, pprev_ref, bcum_ref):
    def proj(lo, width):
        return z[:, lo:lo + width]

    fill()

    c0 = 0
    cb = proj(c0, CONV_DIM)
    cc = proj(c0 + CONV_DIM, CONV_DIM)
    ch = proj(c0 + 2 * CONV_DIM, CONV_DIM)
    u = cc * ch
    uext = jnp.concatenate([uprev_ref[...], u], axis=0)
    u1 = pltpu.roll(uext, 1, axis=0)[CONV_HIST:]
    u2 = pltpu.roll(uext, 2, axis=0)[CONV_HIST:]
    uprev_ref[...] = u[TS - CONV_HIST:]
    cw = convw_ref[layer]
    y_conv = cb * (cw[2:3] * u + cw[1:2] * u1 + cw[0:1] * u2)

    p0 = 3 * CONV_DIM + 2 * REC_FDIM + 2 * REC_DIM
    pu = proj(p0, POOL_DIM)
    pext = jnp.concatenate([pprev_ref[...], pu], axis=0)
    pprev_ref[...] = pu[TS - POOL_HIST:]
    lane = lax.broadcasted_iota(jnp.int32, (TS, POOL_DIM), 1)
    tpos = tile_idx * TS + lax.broadcasted_iota(jnp.int32, (TS, POOL_DIM), 0)
    acc = pext
    wsum = None
    win = None
    span = 1
    for gi, w in enumerate(POOL_WINDOWS):
        while span < w:
            acc = acc + pltpu.roll(acc, span, axis=0)
            span *= 2
        part = acc[POOL_HIST:]
        in_group = lane >= gi * POOL_GROUP
        wsum = part if wsum is None else jnp.where(in_group, part, wsum)
        win = jnp.full((TS, POOL_DIM), w, jnp.int32) if win is None else jnp.where(in_group, w, win)
    count = jnp.minimum(tpos + 1, win).astype(F32)
    pooled = (wsum / count - pu).astype(BF16)
    y_pool = _dot(pooled, poolw_ref[...]) * poolscale_ref[layer:layer + 1, :]
    fill()

    r0 = 3 * CONV_DIM
    rq = proj(r0, REC_FDIM)
    rf = proj(r0 + REC_FDIM, REC_FDIM)
    rv = proj(r0 + 2 * REC_FDIM, REC_DIM)
    rg = proj(r0 + 2 * REC_FDIM + REC_DIM, REC_DIM)

    rl = reclb_ref[...]
    e = jnp.exp(rl - jnp.max(rl, axis=0, keepdims=True))
    sm = e / jnp.sum(e, axis=0, keepdims=True)
    lb = jnp.sum(sm[0:layer + 1], axis=0, keepdims=True) - sm[0:1]

    f = lb + (1.0 - lb) * jax.nn.sigmoid(rf)
    log2f = jnp.log2(f)
    k = 1.0 - f

    row2 = lax.broadcasted_iota(jnp.int32, (TS, TS), 0)
    col2 = lax.broadcasted_iota(jnp.int32, (TS, TS), 1)
    tri = jnp.where(row2 >= col2, 1.0, 0.0).astype(BF16)
    l_hi = log2f.astype(BF16)
    l_lo = (log2f - l_hi.astype(F32)).astype(BF16)
    bcum = _dot(tri, l_hi) + _dot(tri, l_lo)
    bcum_ref[...] = bcum
    b_last = bcum_ref[TS - 1:TS, :]

    q_in = (rq * jnp.exp2(bcum)).astype(BF16)
    k_out = (k * jnp.exp2(b_last - bcum)).astype(BF16)
    decay_all = jnp.exp2(b_last)

    rbit = lax.broadcasted_iota(jnp.int32, (TS, REC_FDIM), 0)
    b3 = bcum.reshape(TS // SUBLANES, SUBLANES, REC_FDIM)
    sub3 = lax.broadcasted_iota(jnp.int32, b3.shape, 1)

    def level_operand(half):
        if half >= SUBLANES:
            parts = []
            for gidx in range(TS // (2 * half)):
                lo = 2 * half * gidx
                ref = bcum_ref[lo + half - 1:lo + half, :]
                parts.append(k[lo:lo + half] * jnp.exp2(ref - bcum[lo:lo + half]))
                parts.append(rq[lo + half:lo + 2 * half]
                             * jnp.exp2(bcum[lo + half:lo + 2 * half] - ref))
            return jnp.concatenate(parts, axis=0).astype(BF16)
        if half == 1:
            ref = jnp.where((sub3 & 1) == 1, pltpu.roll(b3, 1, axis=1), b3)
        elif half == 2:
            m = sub3 & 3
            ref = jnp.where(m == 0, pltpu.roll(b3, SUBLANES - 1, axis=1),
                            jnp.where(m == 1, b3,
                                      jnp.where(m == 2, pltpu.roll(b3, 1, axis=1),
                                                pltpu.roll(b3, 2, axis=1))))
        else:
            ref = jnp.concatenate(
                [jnp.broadcast_to(bcum_ref[SUBLANES * gidx + half - 1:SUBLANES * gidx + half, :],
                                  (1, SUBLANES, REC_FDIM))
                 for gidx in range(TS // SUBLANES)], axis=0)
        gfac = jnp.exp2(-jnp.abs(b3 - ref)).reshape(TS, REC_FDIM)
        return (jnp.where((rbit & half) != 0, rq, k) * gfac).astype(BF16)

    w_lvl = {}
    for half in LEVELS:
        fill()
        w_lvl[half] = level_operand(half)

    trow = lax.broadcasted_iota(jnp.int32, (HALF, HALF), 0)
    tcol = lax.broadcasted_iota(jnp.int32, (HALF, HALF), 1)
    xor = trow ^ tcol
    causal = trow > tcol
    lvl_mask = {half: xor >= half for half in LEVELS[:-1]}
    zero_blk = jnp.zeros((HALF, HALF), F32)

    sg = jax.nn.sigmoid(rg)
    ng = normg_ref[layer:layer + 1, :]
    y_rec = []
    for hd in range(REC_HEADS):
        sl = slice(hd * REC_HEAD_DIM, (hd + 1) * REC_HEAD_DIM)
        qh = rq[:, sl]
        kh = k[:, sl]
        vh = rv[:, sl]
        vb = vh.astype(BF16)
        diag = []
        for d in range(2):
            rows = slice(d * HALF, (d + 1) * HALF)
            blk = zero_blk
            for half in LEVELS[:-1]:
                wl = w_lvl[half][rows, sl]
                blk = jnp.where(lvl_mask[half], _dot_nt(wl, wl), blk)
            diag.append(jnp.where(causal, blk, 0.0))
        wl = w_lvl[HALF]
        cross = _dot_nt(wl[HALF:, sl], wl[:HALF, sl])
        fill()
        attn = jnp.concatenate(
            [jnp.concatenate([diag[0], zero_blk], axis=1),
             jnp.concatenate([cross, diag[1]], axis=1)], axis=0).astype(BF16)
        st = state_ref[hd]
        o = (_dot(attn, vb)
             + jnp.sum(qh * kh, axis=-1, keepdims=True) * vh
             + _dot_nt(q_in[:, sl], st.astype(BF16)))
        state_ref[hd] = st * decay_all[:, sl] + _dot(vh.T.astype(BF16), k_out[:, sl])
        o = o * lax.rsqrt(jnp.mean(o * o, axis=-1, keepdims=True) + RMS_EPS)
        y_rec.append(o * ng[:, sl] * sg[:, sl])

    mix = jnp.concatenate([y_conv] + y_rec + [y_pool], axis=-1).astype(BF16)
    y = _dot(mix, wout_ref[...])
    return _layer_norm(ALPHA * x + y, *_ln_params(g_ref, beta_ref, layer, 1))


def _mixer_kernel(h_ref, hnext_ref, win_ref, convw_ref, reclb_ref, normg_ref, poolw_ref,
                  poolscale_ref, wout_ref, g_ref, beta_ref, o_ref,
                  win_s, wout_s, z0_ref, z1_ref, state_ref, uprev_ref, pprev_ref, bcum_ref,
                  *, layer, steps_per_seq):
    step = pl.program_id(0)

    @pl.when(step < WCHUNKS)
    def _():
        _stage_chunk(step, win_ref, win_s)
        _stage_chunk(step, wout_ref, wout_s)

    @pl.when(step >= WCHUNKS)
    def _():
        s_idx = lax.rem(step - WCHUNKS, steps_per_seq)

        @pl.when(s_idx == 0)
        def _():
            state_ref[...] = jnp.zeros_like(state_ref)
            uprev_ref[...] = jnp.zeros_like(uprev_ref)
            pprev_ref[...] = jnp.zeros_like(pprev_ref)

        @pl.when(step == WCHUNKS)
        def _():
            z0_ref[...] = _dot(h_ref[0:TS].astype(BF16), win_s[...])

        tile_refs = (convw_ref, reclb_ref, normg_ref, poolw_ref, poolscale_ref, wout_s, g_ref,
                     beta_ref, state_ref, uprev_ref, pprev_ref, bcum_ref)

        def projector(xb, dst_ref, plan):
            todo = list(range(0, D_IN, FILL_COLS))
            counts = list(plan)

            def fill(flush=False):
                for _ in range(len(todo) if flush else counts.pop(0)):
                    lo = todo.pop(0)
                    dst_ref[:, lo:lo + FILL_COLS] = _dot(xb, win_s[:, lo:lo + FILL_COLS])

            return fill

        fill = projector(h_ref[TS:2 * TS].astype(BF16), z1_ref, FILL_PLAN_FIRST)
        o_ref[0:TS] = _mixer_tile(h_ref[0:TS], z0_ref[...], 2 * s_idx, layer, fill, *tile_refs)
        fill(flush=True)
        fill = projector(hnext_ref[...].astype(BF16), z0_ref, FILL_PLAN_SECOND)
        o_ref[TS:2 * TS] = _mixer_tile(h_ref[TS:2 * TS], z1_ref[...], 2 * s_idx + 1, layer, fill,
                                       *tile_refs)
        fill(flush=True)


def _mixer(h, batch, layer, w_in, conv_w, rec_lb, norm_g, pool_w, pool_scale, w_out, g, b):
    t = h.shape[0]
    steps_per_seq = t // batch // (2 * TS)
    last_tile = t // TS - 1
    nxt = pl.BlockSpec(
        (TS, D_MODEL),
        lambda i: (jnp.minimum(2 * jnp.maximum(i - WCHUNKS, 0) + 2, last_tile), 0))
    small = (conv_w, rec_lb, norm_g, pool_w, pool_scale)
    return pl.pallas_call(
        functools.partial(_mixer_kernel, layer=layer, steps_per_seq=steps_per_seq),
        grid=(WCHUNKS + batch * steps_per_seq,),
        in_specs=([_token_rows(2 * TS), nxt, _staged(w_in.shape, layer)]
                  + [_resident(p.shape) for p in small]
                  + [_staged(w_out.shape, layer), _resident(g.shape), _resident(b.shape)]),
        out_specs=_token_rows(2 * TS),
        out_shape=jax.ShapeDtypeStruct((t, D_MODEL), F32),
        scratch_shapes=[
            pltpu.VMEM(w_in.shape[1:], BF16),
            pltpu.VMEM(w_out.shape[1:], BF16),
            pltpu.VMEM((TS, D_IN), F32),
            pltpu.VMEM((TS, D_IN), F32),
            pltpu.VMEM((REC_HEADS, REC_HEAD_DIM, REC_HEAD_DIM), F32),
            pltpu.VMEM((CONV_HIST, CONV_DIM), F32),
            pltpu.VMEM((POOL_HIST, POOL_DIM), F32),
            pltpu.VMEM((TS, REC_FDIM), F32),
        ],
        compiler_params=pltpu.CompilerParams(
            dimension_semantics=("arbitrary",), vmem_limit_bytes=VMEM_LIMIT_BYTES),
        name="mixer_ln",
    )(h, h, w_in, *small, w_out, g, b)


def _kv_kernel(mem_ref, wk_ref, wv_ref, k_ref, v_ref):
    mb = mem_ref[...].astype(BF16)
    k_ref[...] = _dot(mb, wk_ref[...].astype(BF16)).astype(BF16)
    v_ref[...] = _dot(mb, wv_ref[...].astype(BF16)).astype(BF16)


def _kv(mem, layer, wk, wv):
    m = mem.shape[0]
    wspec = pl.BlockSpec((None, D_MODEL, KV_COLS), lambda j: (layer, 0, j))
    ospec = pl.BlockSpec((m, KV_COLS), lambda j: (0, j))
    return pl.pallas_call(
        _kv_kernel,
        grid=(D_MODEL // KV_COLS,),
        in_specs=[_resident(mem.shape), wspec, wspec],
        out_specs=[ospec, ospec],
        out_shape=[jax.ShapeDtypeStruct((m, D_MODEL), BF16)] * 2,
        compiler_params=pltpu.CompilerParams(
            dimension_semantics=("parallel",), vmem_limit_bytes=VMEM_LIMIT_BYTES),
        name="mem_kv",
    )(mem, wk, wv)


def _ca_kernel(h_ref, k_ref, v_ref, wq_ref, wo_ref, g_ref, b_ref, o_ref, wq_s, wo_s, *, layer):
    step = pl.program_id(0)
    ln_g, ln_b = _ln_params(g_ref, b_ref, layer, 2)

    @pl.when(step < WCHUNKS)
    def _():
        _stage_chunk(step, wq_ref, wq_s, scale=CA_HEAD_DIM ** -0.5)
        _stage_chunk(step, wo_ref, wo_s)

    nsub = TM // SUB
    head_slices = [slice(hd * CA_HEAD_DIM, (hd + 1) * CA_HEAD_DIM) for hd in range(CA_HEADS)]

    def q_proj(i):
        return _dot(h_ref[i * SUB:(i + 1) * SUB].astype(BF16), wq_s[...]).astype(BF16)

    @pl.when(step >= WCHUNKS)
    def _():
        q = q_proj(0)
        for i in range(nsub):
            rows = slice(i * SUB, (i + 1) * SUB)
            scores = [_dot_nt(q[:, sl], k_ref[:, sl]) for sl in head_slices]
            if i + 1 < nsub:
                q = q_proj(i + 1)
            heads = []
            for s, sl in zip(scores, head_slices):
                p = jnp.exp(s - jnp.max(s, axis=-1, keepdims=True))
                p = p * (1.0 / jnp.sum(p, axis=-1, keepdims=True))
                heads.append(_dot(p.astype(BF16), v_ref[:, sl]))
            o = jnp.concatenate(heads, axis=-1).astype(BF16)
            y = _dot(o, wo_s[...])
            o_ref[rows] = _layer_norm(ALPHA * h_ref[rows] + y, ln_g, ln_b)


def _cross_attn(h, batch, layer, kmem, vmem, wq, wo, g, b):
    t = h.shape[0]
    per_batch = t // batch // TM
    mem = pl.BlockSpec((N_MEM, D_MODEL),
                       lambda i: (jnp.maximum(i - WCHUNKS, 0) // per_batch, 0))
    return pl.pallas_call(
        functools.partial(_ca_kernel, layer=layer),
        grid=(WCHUNKS + t // TM,),
        in_specs=[_token_rows(TM), mem, mem, _staged(wq.shape, layer), _staged(wo.shape, layer),
                  _resident(g.shape), _resident(b.shape)],
        out_specs=_token_rows(TM),
        out_shape=jax.ShapeDtypeStruct((t, D_MODEL), F32),
        scratch_shapes=[pltpu.VMEM(wq.shape[1:], BF16), pltpu.VMEM(wo.shape[1:], BF16)],
        compiler_params=pltpu.CompilerParams(
            dimension_semantics=("arbitrary",), vmem_limit_bytes=VMEM_LIMIT_BYTES),
        name="cross_attn_ln",
    )(h, kmem, vmem, wq, wo, g, b)


def _pool_block_diag(w):
    ng, c, d = w.shape
    out = jnp.zeros((ng * c, ng * d), w.dtype)
    for gi in range(ng):
        out = out.at[gi * c:(gi + 1) * c, gi * d:(gi + 1) * d].set(w[gi])
    return out


def kernel(x, mem, ffn1_gate, ffn1_up, ffn1_down, w_in, conv_w, rec_lb, rec_norm_g, pool_w,
           pool_scale, w_out, ca_q, ca_k, ca_v, ca_o, ffn2_gate, ffn2_up, ffn2_down, ln_g, ln_b):
    batch, seq, _ = x.shape
    h = x.reshape(batch * seq, D_MODEL)
    mem2 = mem.reshape(batch * N_MEM, D_MODEL)
    for l in range(DEPTH):
        h = _ffn(h, l, 0, ffn1_gate, ffn1_up, ffn1_down, ln_g, ln_b)
        h = _mixer(h, batch, l, w_in, conv_w, rec_lb, rec_norm_g,
                   _pool_block_diag(pool_w[l]).astype(BF16), pool_scale, w_out, ln_g, ln_b)
        kmem, vmem = _kv(mem2, l, ca_k, ca_v)
        h = _cross_attn(h, batch, l, kmem, vmem, ca_q, ca_o, ln_g, ln_b)
        h = _ffn(h, l, 3, ffn2_gate, ffn2_up, ffn2_down, ln_g, ln_b)
    return h.reshape(batch, seq, D_MODEL)
```

```python
import functools

import jax
import jax.numpy as jnp
from jax import lax
from jax.experimental import pallas as pl
from jax.experimental.pallas import tpu as pltpu

D_MODEL = 1024
DEPTH = 2
N_MEM = 256
CONV_DIM = 256
CONV_WIDTH = 3
REC_HEADS = 4
REC_DIM = 512
REC_HEAD_DIM = 128
REC_FDIM = 512
POOL_DIM = 256
POOL_WINDOWS = (2, 4, 8, 16)
POOL_GROUP = 64
D_IN = 3 * CONV_DIM + 2 * REC_FDIM + 2 * REC_DIM + POOL_DIM
CA_HEADS = 4
CA_HEAD_DIM = 256
D_FF = 2816
ALPHA = (2.0 * DEPTH) ** 0.25
LN_EPS = 1e-5
RMS_EPS = 1e-6

F32 = jnp.float32
BF16 = jnp.bfloat16

VMEM_LIMIT_BYTES = 56 * 1024 * 1024
MIX_VMEM_LIMIT_BYTES = 60 * 1024 * 1024
SUBLANES = 8
TM = 1024
SUB = 256
WCHUNKS = 8
MIX_WCHUNKS = 16
KV_COLS = 512
TS = 256
HALF = TS // 2
PAIR = 2 * REC_HEAD_DIM
CONV_HIST = 8
POOL_HIST = 16
LEVELS = (1, 2, 4, 8, 16, 32, 64, 128)
FILL_COLS = 256
FILL_PLAN = (3, 2) + (1,) * 8 + (1,) * 4 + (2, 1, 2, 1)

assert HALF == REC_HEAD_DIM and D_FF % FILL_COLS == 0 and D_IN % FILL_COLS == 0


def _layer_norm(y, g, b):
    mu = jnp.mean(y, axis=-1, keepdims=True)
    d = y - mu
    var = jnp.mean(d * d, axis=-1, keepdims=True)
    return d * lax.rsqrt(var + LN_EPS) * g + b


def _dot(a, b):
    return jnp.dot(a, b, preferred_element_type=F32)


def _dot_nt(a, b):
    return lax.dot_general(a, b, (((1,), (1,)), ((), ())), preferred_element_type=F32)


def _resident(shape):
    return pl.BlockSpec(shape, lambda *_: (0,) * len(shape), pipeline_mode=pl.Buffered(1))


def _staged(stacked_shape, layer, nchunks=WCHUNKS):
    _, rows, cols = stacked_shape
    return pl.BlockSpec((None, rows // nchunks, cols),
                        lambda i: (layer, jnp.minimum(i, nchunks - 1), 0))


def _ln_params(g_ref, b_ref, layer, idx):
    return g_ref[layer, idx:idx + 1, :], b_ref[layer, idx:idx + 1, :]


def _token_rows(rows):
    return pl.BlockSpec((rows, D_MODEL), lambda i: (jnp.maximum(i - WCHUNKS, 0), 0))


def _stage_chunk(i, src_ref, dst_ref, scale=None):
    rows = src_ref.shape[0]
    w = src_ref[...] if scale is None else src_ref[...] * scale
    dst_ref[pl.ds(pl.multiple_of(i * rows, rows), rows), :] = w.astype(BF16)


def _ffn_kernel(h_ref, wg_ref, wu_ref, wd_ref, g_ref, b_ref, o_ref, wg_s, wu_s, wd_s,
                *, layer, ln_idx):
    step = pl.program_id(0)
    ln_g, ln_b = _ln_params(g_ref, b_ref, layer, ln_idx)

    @pl.when(step < WCHUNKS)
    def _():
        _stage_chunk(step, wg_ref, wg_s)
        _stage_chunk(step, wu_ref, wu_s)
        _stage_chunk(step, wd_ref, wd_s, scale=0.5)

    def up_stage(i):
        xb = h_ref[i * SUB:(i + 1) * SUB].astype(BF16)
        gate = _dot(xb, wg_s[...])
        up = _dot(xb, wu_s[...])
        return (gate * jax.nn.sigmoid(gate) * up).astype(BF16)

    def down_stage(i, act):
        rows = slice(i * SUB, (i + 1) * SUB)
        y = _dot(act, wd_s[...])
        o_ref[rows] = _layer_norm(ALPHA * h_ref[rows] + y, ln_g, ln_b)

    @pl.when(step >= WCHUNKS)
    def _():
        act = up_stage(0)
        for i in range(1, TM // SUB):
            nxt = up_stage(i)
            down_stage(i - 1, act)
            act = nxt
        down_stage(TM // SUB - 1, act)


def _ffn(h, layer, ln_idx, wg, wu, wd, g, b):
    t = h.shape[0]
    return pl.pallas_call(
        functools.partial(_ffn_kernel, layer=layer, ln_idx=ln_idx),
        grid=(WCHUNKS + t // TM,),
        in_specs=[_token_rows(TM), _staged(wg.shape, layer), _staged(wu.shape, layer),
                  _staged(wd.shape, layer), _resident(g.shape), _resident(b.shape)],
        out_specs=_token_rows(TM),
        out_shape=jax.ShapeDtypeStruct((t, D_MODEL), F32),
        scratch_shapes=[pltpu.VMEM(wg.shape[1:], BF16), pltpu.VMEM(wu.shape[1:], BF16),
                        pltpu.VMEM(wd.shape[1:], BF16)],
        compiler_params=pltpu.CompilerParams(
            dimension_semantics=("arbitrary",), vmem_limit_bytes=VMEM_LIMIT_BYTES),
        name="ffn_ln",
    )(h, wg, wu, wd, g, b)


def _filler(units, plan):
    todo = list(units)
    counts = list(plan)

    def fill(flush=False):
        for _ in range(len(todo) if flush else min(counts.pop(0), len(todo))):
            todo.pop(0)()

    return fill


def _next_block_units(x_ref, wg_s, wu_s, wd_s, win_s, ln_g, ln_b, h1_ref, z_ref):
    st = {}

    def up_unit(c):
        def run():
            if "xb" not in st:
                st["xb"] = x_ref[...].astype(BF16)
                st["acts"] = []
            cols = slice(c * FILL_COLS, (c + 1) * FILL_COLS)
            gate = _dot(st["xb"], wg_s[:, cols])
            up = _dot(st["xb"], wu_s[:, cols])
            st["acts"].append((gate * jax.nn.sigmoid(gate) * up).astype(BF16))
        return run

    def down_unit(c):
        def run():
            if "act" not in st:
                st["act"] = jnp.concatenate(st["acts"], axis=-1)
                st["ys"] = []
            st["ys"].append(_dot(st["act"], wd_s[:, c * FILL_COLS:(c + 1) * FILL_COLS]))
        return run

    def proj_unit(c):
        def run():
            if "hb" not in st:
                y = jnp.concatenate(st["ys"], axis=-1)
                h1 = _layer_norm(ALPHA * x_ref[...] + y, ln_g, ln_b)
                h1_ref[...] = h1
                st["hb"] = h1.astype(BF16)
            cols = slice(c * FILL_COLS, (c + 1) * FILL_COLS)
            z_ref[:, cols] = _dot(st["hb"], win_s[:, cols])
        return run

    return ([up_unit(c) for c in range(D_FF // FILL_COLS)]
            + [down_unit(c) for c in range(D_MODEL // FILL_COLS)]
            + [proj_unit(c) for c in range(D_IN // FILL_COLS)])


def _mixer_tile(x, z, tile_idx, layer, fill, convw_ref, reclb_ref, normg_ref, poolw_ref,
                poolscale_ref, wout_ref, g_ref, beta_ref, state_ref, uprev_ref, pprev_ref, bcum_ref):
    def proj(lo, width):
        return z[:, lo:lo + width]

    fill()

    c0 = 0
    cb = proj(c0, CONV_DIM)
    cc = proj(c0 + CONV_DIM, CONV_DIM)
    ch = proj(c0 + 2 * CONV_DIM, CONV_DIM)
    u = cc * ch
    uext = jnp.concatenate([uprev_ref[...], u], axis=0)
    u1 = pltpu.roll(uext, 1, axis=0)[CONV_HIST:]
    u2 = pltpu.roll(uext, 2, axis=0)[CONV_HIST:]
    uprev_ref[...] = u[TS - CONV_HIST:]
    cw = convw_ref[layer]
    y_conv = cb * (cw[2:3] * u + cw[1:2] * u1 + cw[0:1] * u2)

    p0 = 3 * CONV_DIM + 2 * REC_FDIM + 2 * REC_DIM
    pu = proj(p0, POOL_DIM)
    pext = jnp.concatenate([pprev_ref[...], pu], axis=0)
    pprev_ref[...] = pu[TS - POOL_HIST:]
    lane = lax.broadcasted_iota(jnp.int32, (TS, POOL_DIM), 1)
    tpos = tile_idx * TS + lax.broadcasted_iota(jnp.int32, (TS, POOL_DIM), 0)
    acc = pext
    wsum = None
    win = None
    span = 1
    for gi, w in enumerate(POOL_WINDOWS):
        while span < w:
            acc = acc + pltpu.roll(acc, span, axis=0)
            span *= 2
        part = acc[POOL_HIST:]
        in_group = lane >= gi * POOL_GROUP
        wsum = part if wsum is None else jnp.where(in_group, part, wsum)
        win = jnp.full((TS, POOL_DIM), w, jnp.int32) if win is None else jnp.where(in_group, w, win)
    count = jnp.minimum(tpos + 1, win).astype(F32)
    pooled = (wsum / count - pu).astype(BF16)
    y_pool = _dot(pooled, poolw_ref[...]) * poolscale_ref[layer:layer + 1, :]
    fill()

    r0 = 3 * CONV_DIM
    rq = proj(r0, REC_FDIM)
    rf = proj(r0 + REC_FDIM, REC_FDIM)
    rv = proj(r0 + 2 * REC_FDIM, REC_DIM)
    rg = proj(r0 + 2 * REC_FDIM + REC_DIM, REC_DIM)

    rl = reclb_ref[...]
    e = jnp.exp(rl - jnp.max(rl, axis=0, keepdims=True))
    sm = e / jnp.sum(e, axis=0, keepdims=True)
    lb = jnp.sum(sm[0:layer + 1], axis=0, keepdims=True) - sm[0:1]

    f = lb + (1.0 - lb) * jax.nn.sigmoid(rf)
    log2f = jnp.log2(f)
    k = 1.0 - f

    row2 = lax.broadcasted_iota(jnp.int32, (TS, TS), 0)
    col2 = lax.broadcasted_iota(jnp.int32, (TS, TS), 1)
    tri = jnp.where(row2 >= col2, 1.0, 0.0).astype(BF16)
    l_hi = log2f.astype(BF16)
    l_lo = (log2f - l_hi.astype(F32)).astype(BF16)
    bcum = _dot(tri, l_hi) + _dot(tri, l_lo)
    bcum_ref[...] = bcum
    b_last = bcum_ref[TS - 1:TS, :]

    q_in = (rq * jnp.exp2(bcum)).astype(BF16)
    k_out = (k * jnp.exp2(b_last - bcum)).astype(BF16)
    decay_all = jnp.exp2(b_last)

    rbit = lax.broadcasted_iota(jnp.int32, (TS, REC_FDIM), 0)
    b3 = bcum.reshape(TS // SUBLANES, SUBLANES, REC_FDIM)
    sub3 = lax.broadcasted_iota(jnp.int32, b3.shape, 1)

    def level_operand(half):
        if half >= SUBLANES:
            parts = []
            for gidx in range(TS // (2 * half)):
                lo = 2 * half * gidx
                ref = bcum_ref[lo + half - 1:lo + half, :]
                parts.append(k[lo:lo + half] * jnp.exp2(ref - bcum[lo:lo + half]))
                parts.append(rq[lo + half:lo + 2 * half]
                             * jnp.exp2(bcum[lo + half:lo + 2 * half] - ref))
            return jnp.concatenate(parts, axis=0).astype(BF16)
        if half == 1:
            ref = jnp.where((sub3 & 1) == 1, pltpu.roll(b3, 1, axis=1), b3)
        elif half == 2:
            m = sub3 & 3
            ref = jnp.where(m == 0, pltpu.roll(b3, SUBLANES - 1, axis=1),
                            jnp.where(m == 1, b3,
                                      jnp.where(m == 2, pltpu.roll(b3, 1, axis=1),
                                                pltpu.roll(b3, 2, axis=1))))
        else:
            ref = jnp.concatenate(
                [jnp.broadcast_to(bcum_ref[SUBLANES * gidx + half - 1:SUBLANES * gidx + half, :],
                                  (1, SUBLANES, REC_FDIM))
                 for gidx in range(TS // SUBLANES)], axis=0)
        gfac = jnp.exp2(-jnp.abs(b3 - ref)).reshape(TS, REC_FDIM)
        return (jnp.where((rbit & half) != 0, rq, k) * gfac).astype(BF16)

    w_lvl = {}
    for half in LEVELS:
        fill()
        w_lvl[half] = level_operand(half)

    trow = lax.broadcasted_iota(jnp.int32, (HALF, PAIR), 0)
    tcol = lax.broadcasted_iota(jnp.int32, (HALF, PAIR), 1) & (HALF - 1)
    xor = trow ^ tcol
    causal = trow > tcol
    lvl_mask = {half: xor >= half for half in LEVELS[:-1]}
    zero_pair = jnp.zeros((HALF, PAIR), F32)
    zero_blk = jnp.zeros((HALF, HALF), F32)
    zero_key = jnp.zeros((HALF, REC_HEAD_DIM), BF16)

    def pair_nt(lhs, keys):
        rhs = jnp.concatenate(
            [jnp.concatenate([keys[:, :REC_HEAD_DIM], zero_key], axis=1),
             jnp.concatenate([zero_key, keys[:, REC_HEAD_DIM:]], axis=1)], axis=0)
        return _dot_nt(lhs, rhs)

    attn = []
    for pr in range(REC_HEADS // 2):
        lanes = slice(pr * PAIR, (pr + 1) * PAIR)
        diag = []
        for d in range(2):
            rows = slice(d * HALF, (d + 1) * HALF)
            blk = zero_pair
            for half in LEVELS[:-1]:
                wl = w_lvl[half][rows, lanes]
                blk = jnp.where(lvl_mask[half], pair_nt(wl, wl), blk)
            diag.append(jnp.where(causal, blk, 0.0))
            fill()
        wl = w_lvl[HALF]
        cross = pair_nt(wl[HALF:, lanes], wl[:HALF, lanes])
        for j in range(2):
            cs = slice(j * HALF, (j + 1) * HALF)
            attn.append(jnp.concatenate(
                [jnp.concatenate([diag[0][:, cs], zero_blk], axis=1),
                 jnp.concatenate([cross[:, cs], diag[1][:, cs]], axis=1)], axis=0).astype(BF16))

    sg = jax.nn.sigmoid(rg)
    ng = normg_ref[layer:layer + 1, :]
    y_rec = []
    for hd in range(REC_HEADS):
        fill()
        sl = slice(hd * REC_HEAD_DIM, (hd + 1) * REC_HEAD_DIM)
        qh = rq[:, sl]
        kh = k[:, sl]
        vh = rv[:, sl]
        st = state_ref[hd]
        o = (_dot(attn[hd], vh.astype(BF16))
             + jnp.sum(qh * kh, axis=-1, keepdims=True) * vh
             + _dot_nt(q_in[:, sl], st.astype(BF16)))
        state_ref[hd] = st * decay_all[:, sl] + _dot(vh.T.astype(BF16), k_out[:, sl])
        o = o * lax.rsqrt(jnp.mean(o * o, axis=-1, keepdims=True) + RMS_EPS)
        y_rec.append(o * ng[:, sl] * sg[:, sl])

    mix = jnp.concatenate([y_conv] + y_rec + [y_pool], axis=-1).astype(BF16)
    y = _dot(mix, wout_ref[...])
    return _layer_norm(ALPHA * x + y, *_ln_params(g_ref, beta_ref, layer, 1))


def _ffn_mixer_kernel(hfirst_ref, hodd_ref, hnext_ref, wg_ref, wu_ref, wd_ref, win_ref, convw_ref,
                      reclb_ref, normg_ref, poolw_ref, poolscale_ref, wout_ref, g_ref, beta_ref,
                      o_ref,
                      wg_s, wu_s, wd_s, win_s, wout_s, h1a_ref, h1b_ref, za_ref, zb_ref,
                      state_ref, uprev_ref, pprev_ref, bcum_ref, *, layer, steps_per_seq):
    step = pl.program_id(0)

    @pl.when(step < MIX_WCHUNKS)
    def _():
        _stage_chunk(step, wg_ref, wg_s)
        _stage_chunk(step, wu_ref, wu_s)
        _stage_chunk(step, wd_ref, wd_s, scale=0.5)
        _stage_chunk(step, win_ref, win_s)
        _stage_chunk(step, wout_ref, wout_s)

    @pl.when(step >= MIX_WCHUNKS)
    def _():
        s_idx = lax.rem(step - MIX_WCHUNKS, steps_per_seq)
        ln_g, ln_b = _ln_params(g_ref, beta_ref, layer, 0)

        def upcoming(x_ref, h1_ref, z_ref):
            return _next_block_units(x_ref, wg_s, wu_s, wd_s, win_s, ln_g, ln_b, h1_ref, z_ref)

        @pl.when(s_idx == 0)
        def _():
            state_ref[...] = jnp.zeros_like(state_ref)
            uprev_ref[...] = jnp.zeros_like(uprev_ref)
            pprev_ref[...] = jnp.zeros_like(pprev_ref)

        @pl.when(step == MIX_WCHUNKS)
        def _():
            _filler(upcoming(hfirst_ref, h1a_ref, za_ref), ())(flush=True)

        tile_refs = (convw_ref, reclb_ref, normg_ref, poolw_ref, poolscale_ref, wout_s, g_ref,
                     beta_ref, state_ref, uprev_ref, pprev_ref, bcum_ref)

        fill = _filler(upcoming(hodd_ref, h1b_ref, zb_ref), FILL_PLAN)
        o_ref[0:TS] = _mixer_tile(h1a_ref[...], za_ref[...], 2 * s_idx, layer, fill, *tile_refs)
        fill(flush=True)
        fill = _filler(upcoming(hnext_ref, h1a_ref, za_ref), FILL_PLAN)
        o_ref[TS:2 * TS] = _mixer_tile(h1b_ref[...], zb_ref[...], 2 * s_idx + 1, layer, fill,
                                       *tile_refs)
        fill(flush=True)


def _ffn_mixer(h, batch, layer, wg, wu, wd, w_in, conv_w, rec_lb, norm_g, pool_w, pool_scale,
               w_out, g, b):
    t = h.shape[0]
    steps_per_seq = t // batch // (2 * TS)
    last_block = t // TS - 1

    def block(index_of_step):
        return pl.BlockSpec(
            (TS, D_MODEL), lambda i: (index_of_step(jnp.maximum(i - MIX_WCHUNKS, 0)), 0))

    hodd = block(lambda s: 2 * s + 1)
    hnext = block(lambda s: jnp.minimum(2 * s + 2, last_block))
    hfirst = pl.BlockSpec((TS, D_MODEL), lambda i: (0, 0), pipeline_mode=pl.Buffered(1))
    out = pl.BlockSpec((2 * TS, D_MODEL), lambda i: (jnp.maximum(i - MIX_WCHUNKS, 0), 0))
    staged = lambda w: _staged(w.shape, layer, MIX_WCHUNKS)
    small = (conv_w, rec_lb, norm_g, pool_w, pool_scale)
    return pl.pallas_call(
        functools.partial(_ffn_mixer_kernel, layer=layer, steps_per_seq=steps_per_seq),
        grid=(MIX_WCHUNKS + batch * steps_per_seq,),
        in_specs=([hfirst, hodd, hnext, staged(wg), staged(wu), staged(wd), staged(w_in)]
                  + [_resident(p.shape) for p in small]
                  + [staged(w_out), _resident(g.shape), _resident(b.shape)]),
        out_specs=out,
        out_shape=jax.ShapeDtypeStruct((t, D_MODEL), F32),
        scratch_shapes=[
            pltpu.VMEM(wg.shape[1:], BF16),
            pltpu.VMEM(wu.shape[1:], BF16),
            pltpu.VMEM(wd.shape[1:], BF16),
            pltpu.VMEM(w_in.shape[1:], BF16),
            pltpu.VMEM(w_out.shape[1:], BF16),
            pltpu.VMEM((TS, D_MODEL), F32),
            pltpu.VMEM((TS, D_MODEL), F32),
            pltpu.VMEM((TS, D_IN), F32),
            pltpu.VMEM((TS, D_IN), F32),
            pltpu.VMEM((REC_HEADS, REC_HEAD_DIM, REC_HEAD_DIM), F32),
            pltpu.VMEM((CONV_HIST, CONV_DIM), F32),
            pltpu.VMEM((POOL_HIST, POOL_DIM), F32),
            pltpu.VMEM((TS, REC_FDIM), F32),
        ],
        compiler_params=pltpu.CompilerParams(
            dimension_semantics=("arbitrary",), vmem_limit_bytes=MIX_VMEM_LIMIT_BYTES),
        name="ffn_mixer_ln",
    )(h, h, h, wg, wu, wd, w_in, *small, w_out, g, b)


def _kv_kernel(mem_ref, wk_ref, wv_ref, k_ref, v_ref):
    mb = mem_ref[...].astype(BF16)
    k_ref[...] = _dot(mb, wk_ref[...].astype(BF16)).astype(BF16)
    v_ref[...] = _dot(mb, wv_ref[...].astype(BF16)).astype(BF16)


def _kv(mem, layer, wk, wv):
    m = mem.shape[0]
    wspec = pl.BlockSpec((None, D_MODEL, KV_COLS), lambda j: (layer, 0, j))
    ospec = pl.BlockSpec((m, KV_COLS), lambda j: (0, j))
    return pl.pallas_call(
        _kv_kernel,
        grid=(D_MODEL // KV_COLS,),
        in_specs=[_resident(mem.shape), wspec, wspec],
        out_specs=[ospec, ospec],
        out_shape=[jax.ShapeDtypeStruct((m, D_MODEL), BF16)] * 2,
        compiler_params=pltpu.CompilerParams(
            dimension_semantics=("parallel",), vmem_limit_bytes=VMEM_LIMIT_BYTES),
        name="mem_kv",
    )(mem, wk, wv)


def _ca_kernel(h_ref, k_ref, v_ref, wq_ref, wo_ref, g_ref, b_ref, o_ref, wq_s, wo_s, *, layer):
    step = pl.program_id(0)
    ln_g, ln_b = _ln_params(g_ref, b_ref, layer, 2)

    @pl.when(step < WCHUNKS)
    def _():
        _stage_chunk(step, wq_ref, wq_s, scale=CA_HEAD_DIM ** -0.5)
        _stage_chunk(step, wo_ref, wo_s)

    nsub = TM // SUB
    head_slices = [slice(hd * CA_HEAD_DIM, (hd + 1) * CA_HEAD_DIM) for hd in range(CA_HEADS)]

    def q_proj(i):
        return _dot(h_ref[i * SUB:(i + 1) * SUB].astype(BF16), wq_s[...]).astype(BF16)

    @pl.when(step >= WCHUNKS)
    def _():
        q = q_proj(0)
        for i in range(nsub):
            rows = slice(i * SUB, (i + 1) * SUB)
            scores = [_dot_nt(q[:, sl], k_ref[:, sl]) for sl in head_slices]
            if i + 1 < nsub:
                q = q_proj(i + 1)
            heads = []
            for s, sl in zip(scores, head_slices):
                p = jnp.exp(s - jnp.max(s, axis=-1, keepdims=True))
                p = p * (1.0 / jnp.sum(p, axis=-1, keepdims=True))
                heads.append(_dot(p.astype(BF16), v_ref[:, sl]))
            o = jnp.concatenate(heads, axis=-1).astype(BF16)
            y = _dot(o, wo_s[...])
            o_ref[rows] = _layer_norm(ALPHA * h_ref[rows] + y, ln_g, ln_b)


def _cross_attn(h, batch, layer, kmem, vmem, wq, wo, g, b):
    t = h.shape[0]
    per_batch = t // batch // TM
    mem = pl.BlockSpec((N_MEM, D_MODEL),
                       lambda i: (jnp.maximum(i - WCHUNKS, 0) // per_batch, 0))
    return pl.pallas_call(
        functools.partial(_ca_kernel, layer=layer),
        grid=(WCHUNKS + t // TM,),
        in_specs=[_token_rows(TM), mem, mem, _staged(wq.shape, layer), _staged(wo.shape, layer),
                  _resident(g.shape), _resident(b.shape)],
        out_specs=_token_rows(TM),
        out_shape=jax.ShapeDtypeStruct((t, D_MODEL), F32),
        scratch_shapes=[pltpu.VMEM(wq.shape[1:], BF16), pltpu.VMEM(wo.shape[1:], BF16)],
        compiler_params=pltpu.CompilerParams(
            dimension_semantics=("arbitrary",), vmem_limit_bytes=VMEM_LIMIT_BYTES),
        name="cross_attn_ln",
    )(h, kmem, vmem, wq, wo, g, b)


def _pool_block_diag(w):
    ng, c, d = w.shape
    out = jnp.zeros((ng * c, ng * d), w.dtype)
    for gi in range(ng):
        out = out.at[gi * c:(gi + 1) * c, gi * d:(gi + 1) * d].set(w[gi])
    return out


def kernel(x, mem, ffn1_gate, ffn1_up, ffn1_down, w_in, conv_w, rec_lb, rec_norm_g, pool_w,
           pool_scale, w_out, ca_q, ca_k, ca_v, ca_o, ffn2_gate, ffn2_up, ffn2_down, ln_g, ln_b):
    batch, seq, _ = x.shape
    h = x.reshape(batch * seq, D_MODEL)
    mem2 = mem.reshape(batch * N_MEM, D_MODEL)
    for l in range(DEPTH):
        h = _ffn_mixer(h, batch, l, ffn1_gate, ffn1_up, ffn1_down, w_in, conv_w, rec_lb,
                       rec_norm_g, _pool_block_diag(pool_w[l]).astype(BF16), pool_scale, w_out,
                       ln_g, ln_b)
        kmem, vmem = _kv(mem2, l, ca_k, ca_v)
        h = _cross_attn(h, batch, l, kmem, vmem, ca_q, ca_o, ln_g, ln_b)
        h = _ffn(h, l, 3, ffn2_gate, ffn2_up, ffn2_down, ln_g, ln_b)
    return h.reshape(batch, seq, D_MODEL)
```

```python
import functools

import jax
import jax.numpy as jnp
from jax import lax
from jax.experimental import pallas as pl
from jax.experimental.pallas import tpu as pltpu

D_MODEL = 1024
DEPTH = 2
N_MEM = 256
CONV_DIM = 256
CONV_WIDTH = 3
REC_HEADS = 4
REC_DIM = 512
REC_HEAD_DIM = 128
REC_FDIM = 512
POOL_DIM = 256
POOL_WINDOWS = (2, 4, 8, 16)
POOL_GROUP = 64
D_IN = 3 * CONV_DIM + 2 * REC_FDIM + 2 * REC_DIM + POOL_DIM
CA_HEADS = 4
CA_HEAD_DIM = 256
D_FF = 2816
ALPHA = (2.0 * DEPTH) ** 0.25
LN_EPS = 1e-5
RMS_EPS = 1e-6

F32 = jnp.float32
BF16 = jnp.bfloat16

VMEM_LIMIT_BYTES = 56 * 1024 * 1024
SUBLANES = 8
TM = 1024
SUB = 256
WCHUNKS = 8
KV_COLS = 512
TS = 256
HALF = TS // 2
PAIR = 2 * REC_HEAD_DIM
CONV_HIST = 8
POOL_HIST = 16
LEVELS = (1, 2, 4, 8, 16, 32, 64, 128)
FILL_COLS = 256
FILL_PLAN_FIRST = (2, 1) + (1, 1, 1, 1, 1, 0, 0, 0) + (0, 0, 0, 0) + (1, 1, 1, 1)
FILL_PLAN_SECOND = (0, 0) + (1, 1, 1, 1, 1, 1, 1, 1) + (0, 0, 0, 0) + (1, 1, 1, 1)

assert HALF == REC_HEAD_DIM and D_IN % FILL_COLS == 0


def _layer_norm(y, g, b):
    mu = jnp.mean(y, axis=-1, keepdims=True)
    d = y - mu
    var = jnp.mean(d * d, axis=-1, keepdims=True)
    return d * lax.rsqrt(var + LN_EPS) * g + b


def _dot(a, b):
    return jnp.dot(a, b, preferred_element_type=F32)


def _dot_nt(a, b):
    return lax.dot_general(a, b, (((1,), (1,)), ((), ())), preferred_element_type=F32)


def _resident(shape):
    return pl.BlockSpec(shape, lambda *_: (0,) * len(shape), pipeline_mode=pl.Buffered(1))


def _staged(stacked_shape, layer, nchunks=WCHUNKS):
    _, rows, cols = stacked_shape
    return pl.BlockSpec((None, rows // nchunks, cols),
                        lambda i: (layer, jnp.minimum(i, nchunks - 1), 0))


def _ln_params(g_ref, b_ref, layer, idx):
    return g_ref[layer, idx:idx + 1, :], b_ref[layer, idx:idx + 1, :]


def _token_rows(rows):
    return pl.BlockSpec((rows, D_MODEL), lambda i: (jnp.maximum(i - WCHUNKS, 0), 0))


def _stage_chunk(i, src_ref, dst_ref, scale=None):
    rows = src_ref.shape[0]
    w = src_ref[...] if scale is None else src_ref[...] * scale
    dst_ref[pl.ds(pl.multiple_of(i * rows, rows), rows), :] = w.astype(BF16)


def _ffn_kernel(h_ref, wg_ref, wu_ref, wd_ref, g_ref, b_ref, o_ref, wg_s, wu_s, wd_s,
                *, layer, ln_idx):
    step = pl.program_id(0)
    ln_g, ln_b = _ln_params(g_ref, b_ref, layer, ln_idx)

    @pl.when(step < WCHUNKS)
    def _():
        _stage_chunk(step, wg_ref, wg_s)
        _stage_chunk(step, wu_ref, wu_s)
        _stage_chunk(step, wd_ref, wd_s, scale=0.5)

    def up_stage(i):
        xb = h_ref[i * SUB:(i + 1) * SUB].astype(BF16)
        gate = _dot(xb, wg_s[...])
        up = _dot(xb, wu_s[...])
        return (gate * jax.nn.sigmoid(gate) * up).astype(BF16)

    def down_stage(i, act):
        rows = slice(i * SUB, (i + 1) * SUB)
        y = _dot(act, wd_s[...])
        o_ref[rows] = _layer_norm(ALPHA * h_ref[rows] + y, ln_g, ln_b)

    @pl.when(step >= WCHUNKS)
    def _():
        act = up_stage(0)
        for i in range(1, TM // SUB):
            nxt = up_stage(i)
            down_stage(i - 1, act)
            act = nxt
        down_stage(TM // SUB - 1, act)


def _ffn(h, layer, ln_idx, wg, wu, wd, g, b):
    t = h.shape[0]
    return pl.pallas_call(
        functools.partial(_ffn_kernel, layer=layer, ln_idx=ln_idx),
        grid=(WCHUNKS + t // TM,),
        in_specs=[_token_rows(TM), _staged(wg.shape, layer), _staged(wu.shape, layer),
                  _staged(wd.shape, layer), _resident(g.shape), _resident(b.shape)],
        out_specs=_token_rows(TM),
        out_shape=jax.ShapeDtypeStruct((t, D_MODEL), F32),
        scratch_shapes=[pltpu.VMEM(wg.shape[1:], BF16), pltpu.VMEM(wu.shape[1:], BF16),
                        pltpu.VMEM(wd.shape[1:], BF16)],
        compiler_params=pltpu.CompilerParams(
            dimension_semantics=("arbitrary",), vmem_limit_bytes=VMEM_LIMIT_BYTES),
        name="ffn_ln",
    )(h, wg, wu, wd, g, b)


def _mixer_tile(x, z, tile_idx, layer, fill, convw_ref, reclb_ref, normg_ref, poolw_ref,
                poolscale_ref, wout_ref, g_ref, beta_ref, state_ref, uprev_ref, pprev_ref, bcum_ref):
    def proj(lo, width):
        return z[:, lo:lo + width]

    fill()

    c0 = 0
    cb = proj(c0, CONV_DIM)
    cc = proj(c0 + CONV_DIM, CONV_DIM)
    ch = proj(c0 + 2 * CONV_DIM, CONV_DIM)
    u = cc * ch
    uext = jnp.concatenate([uprev_ref[...], u], axis=0)
    u1 = pltpu.roll(uext, 1, axis=0)[CONV_HIST:]
    u2 = pltpu.roll(uext, 2, axis=0)[CONV_HIST:]
    uprev_ref[...] = u[TS - CONV_HIST:]
    cw = convw_ref[layer]
    y_conv = cb * (cw[2:3] * u + cw[1:2] * u1 + cw[0:1] * u2)

    p0 = 3 * CONV_DIM + 2 * REC_FDIM + 2 * REC_DIM
    pu = proj(p0, POOL_DIM)
    pext = jnp.concatenate([pprev_ref[...], pu], axis=0)
    pprev_ref[...] = pu[TS - POOL_HIST:]
    lane = lax.broadcasted_iota(jnp.int32, (TS, POOL_DIM), 1)
    tpos = tile_idx * TS + lax.broadcasted_iota(jnp.int32, (TS, POOL_DIM), 0)
    acc = pext
    wsum = None
    win = None
    span = 1
    for gi, w in enumerate(POOL_WINDOWS):
        while span < w:
            acc = acc + pltpu.roll(acc, span, axis=0)
            span *= 2
        part = acc[POOL_HIST:]
        in_group = lane >= gi * POOL_GROUP
        wsum = part if wsum is None else jnp.where(in_group, part, wsum)
        win = jnp.full((TS, POOL_DIM), w, jnp.int32) if win is None else jnp.where(in_group, w, win)
    count = jnp.minimum(tpos + 1, win).astype(F32)
    pooled = (wsum / count - pu).astype(BF16)
    y_pool = _dot(pooled, poolw_ref[...]) * poolscale_ref[layer:layer + 1, :]
    fill()

    r0 = 3 * CONV_DIM
    rq = proj(r0, REC_FDIM)
    rf = proj(r0 + REC_FDIM, REC_FDIM)
    rv = proj(r0 + 2 * REC_FDIM, REC_DIM)
    rg = proj(r0 + 2 * REC_FDIM + REC_DIM, REC_DIM)

    rl = reclb_ref[...]
    e = jnp.exp(rl - jnp.max(rl, axis=0, keepdims=True))
    sm = e / jnp.sum(e, axis=0, keepdims=True)
    lb = jnp.sum(sm[0:layer + 1], axis=0, keepdims=True) - sm[0:1]

    f = lb + (1.0 - lb) * jax.nn.sigmoid(rf)
    log2f = jnp.log2(f)
    k = 1.0 - f

    row2 = lax.broadcasted_iota(jnp.int32, (TS, TS), 0)
    col2 = lax.broadcasted_iota(jnp.int32, (TS, TS), 1)
    tri = jnp.where(row2 >= col2, 1.0, 0.0).astype(BF16)
    l_hi = log2f.astype(BF16)
    l_lo = (log2f - l_hi.astype(F32)).astype(BF16)
    bcum = _dot(tri, l_hi) + _dot(tri, l_lo)
    bcum_ref[...] = bcum
    b_last = bcum_ref[TS - 1:TS, :]

    q_in = (rq * jnp.exp2(bcum)).astype(BF16)
    k_out = (k * jnp.exp2(b_last - bcum)).astype(BF16)
    decay_all = jnp.exp2(b_last)

    rbit = lax.broadcasted_iota(jnp.int32, (TS, REC_FDIM), 0)
    b3 = bcum.reshape(TS // SUBLANES, SUBLANES, REC_FDIM)
    sub3 = lax.broadcasted_iota(jnp.int32, b3.shape, 1)

    def level_operand(half):
        if half >= SUBLANES:
            parts = []
            for gidx in range(TS // (2 * half)):
                lo = 2 * half * gidx
                ref = bcum_ref[lo + half - 1:lo + half, :]
                parts.append(k[lo:lo + half] * jnp.exp2(ref - bcum[lo:lo + half]))
                parts.append(rq[lo + half:lo + 2 * half]
                             * jnp.exp2(bcum[lo + half:lo + 2 * half] - ref))
            return jnp.concatenate(parts, axis=0).astype(BF16)
        if half == 1:
            ref = jnp.where((sub3 & 1) == 1, pltpu.roll(b3, 1, axis=1), b3)
        elif half == 2:
            m = sub3 & 3
            ref = jnp.where(m == 0, pltpu.roll(b3, SUBLANES - 1, axis=1),
                            jnp.where(m == 1, b3,
                                      jnp.where(m == 2, pltpu.roll(b3, 1, axis=1),
                                                pltpu.roll(b3, 2, axis=1))))
        else:
            ref = jnp.concatenate(
                [jnp.broadcast_to(bcum_ref[SUBLANES * gidx + half - 1:SUBLANES * gidx + half, :],
                                  (1, SUBLANES, REC_FDIM))
                 for gidx in range(TS // SUBLANES)], axis=0)
        gfac = jnp.exp2(-jnp.abs(b3 - ref)).reshape(TS, REC_FDIM)
        return (jnp.where((rbit & half) != 0, rq, k) * gfac).astype(BF16)

    w_lvl = {}
    for half in LEVELS:
        fill()
        w_lvl[half] = level_operand(half)

    trow = lax.broadcasted_iota(jnp.int32, (HALF, PAIR), 0)
    tcol = lax.broadcasted_iota(jnp.int32, (HALF, PAIR), 1) & (HALF - 1)
    xor = trow ^ tcol
    causal = trow > tcol
    lvl_mask = {half: xor >= half for half in LEVELS[:-1]}
    zero_pair = jnp.zeros((HALF, PAIR), F32)
    zero_blk = jnp.zeros((HALF, HALF), F32)
    zero_key = jnp.zeros((HALF, REC_HEAD_DIM), BF16)

    def pair_nt(lhs, keys):
        rhs = jnp.concatenate(
            [jnp.concatenate([keys[:, :REC_HEAD_DIM], zero_key], axis=1),
             jnp.concatenate([zero_key, keys[:, REC_HEAD_DIM:]], axis=1)], axis=0)
        return _dot_nt(lhs, rhs)

    attn = []
    for pr in range(REC_HEADS // 2):
        lanes = slice(pr * PAIR, (pr + 1) * PAIR)
        diag = []
        for d in range(2):
            rows = slice(d * HALF, (d + 1) * HALF)
            blk = zero_pair
            for half in LEVELS[:-1]:
                wl = w_lvl[half][rows, lanes]
                blk = jnp.where(lvl_mask[half], pair_nt(wl, wl), blk)
            diag.append(jnp.where(causal, blk, 0.0))
            fill()
        wl = w_lvl[HALF]
        cross = pair_nt(wl[HALF:, lanes], wl[:HALF, lanes])
        for j in range(2):
            cs = slice(j * HALF, (j + 1) * HALF)
            attn.append(jnp.concatenate(
                [jnp.concatenate([diag[0][:, cs], zero_blk], axis=1),
                 jnp.concatenate([cross[:, cs], diag[1][:, cs]], axis=1)], axis=0).astype(BF16))

    sg = jax.nn.sigmoid(rg)
    ng = normg_ref[layer:layer + 1, :]
    y_rec = []
    for hd in range(REC_HEADS):
        fill()
        sl = slice(hd * REC_HEAD_DIM, (hd + 1) * REC_HEAD_DIM)
        qh = rq[:, sl]
        kh = k[:, sl]
        vh = rv[:, sl]
        st = state_ref[hd]
        o = (_dot(attn[hd], vh.astype(BF16))
             + jnp.sum(qh * kh, axis=-1, keepdims=True) * vh
             + _dot_nt(q_in[:, sl], st.astype(BF16)))
        state_ref[hd] = st * decay_all[:, sl] + _dot(vh.T.astype(BF16), k_out[:, sl])
        o = o * lax.rsqrt(jnp.mean(o * o, axis=-1, keepdims=True) + RMS_EPS)
        y_rec.append(o * ng[:, sl] * sg[:, sl])

    mix = jnp.concatenate([y_conv] + y_rec + [y_pool], axis=-1).astype(BF16)
    y = _dot(mix, wout_ref[...])
    return _layer_norm(ALPHA * x + y, *_ln_params(g_ref, beta_ref, layer, 1))


def _mixer_kernel(h_ref, hnext_ref, win_ref, convw_ref, reclb_ref, normg_ref, poolw_ref,
                  poolscale_ref, wout_ref, g_ref, beta_ref, o_ref,
                  win_s, wout_s, z0_ref, z1_ref, state_ref, uprev_ref, pprev_ref, bcum_ref,
                  *, layer, steps_per_seq):
    step = pl.program_id(0)

    @pl.when(step < WCHUNKS)
    def _():
        _stage_chunk(step, win_ref, win_s)
        _stage_chunk(step, wout_ref, wout_s)

    @pl.when(step >= WCHUNKS)
    def _():
        s_idx = lax.rem(step - WCHUNKS, steps_per_seq)

        @pl.when(s_idx == 0)
        def _():
            state_ref[...] = jnp.zeros_like(state_ref)
            uprev_ref[...] = jnp.zeros_like(uprev_ref)
            pprev_ref[...] = jnp.zeros_like(pprev_ref)

        @pl.when(step == WCHUNKS)
        def _():
            z0_ref[...] = _dot(h_ref[0:TS].astype(BF16), win_s[...])

        tile_refs = (convw_ref, reclb_ref, normg_ref, poolw_ref, poolscale_ref, wout_s, g_ref,
                     beta_ref, state_ref, uprev_ref, pprev_ref, bcum_ref)

        def projector(xb, dst_ref, plan):
            todo = list(range(0, D_IN, FILL_COLS))
            counts = list(plan)

            def fill(flush=False):
                for _ in range(len(todo) if flush else counts.pop(0)):
                    lo = todo.pop(0)
                    dst_ref[:, lo:lo + FILL_COLS] = _dot(xb, win_s[:, lo:lo + FILL_COLS])

            return fill

        fill = projector(h_ref[TS:2 * TS].astype(BF16), z1_ref, FILL_PLAN_FIRST)
        o_ref[0:TS] = _mixer_tile(h_ref[0:TS], z0_ref[...], 2 * s_idx, layer, fill, *tile_refs)
        fill(flush=True)
        fill = projector(hnext_ref[...].astype(BF16), z0_ref, FILL_PLAN_SECOND)
        o_ref[TS:2 * TS] = _mixer_tile(h_ref[TS:2 * TS], z1_ref[...], 2 * s_idx + 1, layer, fill,
                                       *tile_refs)
        fill(flush=True)


def _mixer(h, batch, layer, w_in, conv_w, rec_lb, norm_g, pool_w, pool_scale, w_out, g, b):
    t = h.shape[0]
    steps_per_seq = t // batch // (2 * TS)
    last_tile = t // TS - 1
    nxt = pl.BlockSpec(
        (TS, D_MODEL),
        lambda i: (jnp.minimum(2 * jnp.maximum(i - WCHUNKS, 0) + 2, last_tile), 0))
    small = (conv_w, rec_lb, norm_g, pool_w, pool_scale)
    return pl.pallas_call(
        functools.partial(_mixer_kernel, layer=layer, steps_per_seq=steps_per_seq),
        grid=(WCHUNKS + batch * steps_per_seq,),
        in_specs=([_token_rows(2 * TS), nxt, _staged(w_in.shape, layer)]
                  + [_resident(p.shape) for p in small]
                  + [_staged(w_out.shape, layer), _resident(g.shape), _resident(b.shape)]),
        out_specs=_token_rows(2 * TS),
        out_shape=jax.ShapeDtypeStruct((t, D_MODEL), F32),
        scratch_shapes=[
            pltpu.VMEM(w_in.shape[1:], BF16),
            pltpu.VMEM(w_out.shape[1:], BF16),
            pltpu.VMEM((TS, D_IN), F32),
            pltpu.VMEM((TS, D_IN), F32),
            pltpu.VMEM((REC_HEADS, REC_HEAD_DIM, REC_HEAD_DIM), F32),
            pltpu.VMEM((CONV_HIST, CONV_DIM), F32),
            pltpu.VMEM((POOL_HIST, POOL_DIM), F32),
            pltpu.VMEM((TS, REC_FDIM), F32),
        ],
        compiler_params=pltpu.CompilerParams(
            dimension_semantics=("arbitrary",), vmem_limit_bytes=VMEM_LIMIT_BYTES),
        name="mixer_ln",
    )(h, h, w_in, *small, w_out, g, b)


def _kv_kernel(mem_ref, wk_ref, wv_ref, kt_ref, v_ref):
    mb = mem_ref[...].astype(BF16)
    kt_ref[...] = _dot(mb, wk_ref[...].astype(BF16)).T.astype(BF16)
    v_ref[...] = _dot(mb, wv_ref[...].astype(BF16)).astype(BF16)


def _kv(mem, layer, wk, wv):
    m = mem.shape[0]
    wspec = pl.BlockSpec((None, D_MODEL, KV_COLS), lambda j: (layer, 0, j))
    return pl.pallas_call(
        _kv_kernel,
        grid=(D_MODEL // KV_COLS,),
        in_specs=[_resident(mem.shape), wspec, wspec],
        out_specs=[pl.BlockSpec((KV_COLS, m), lambda j: (j, 0)),
                   pl.BlockSpec((m, KV_COLS), lambda j: (0, j))],
        out_shape=[jax.ShapeDtypeStruct((D_MODEL, m), BF16),
                   jax.ShapeDtypeStruct((m, D_MODEL), BF16)],
        compiler_params=pltpu.CompilerParams(
            dimension_semantics=("parallel",), vmem_limit_bytes=VMEM_LIMIT_BYTES),
        name="mem_kv",
    )(mem, wk, wv)


def _ca_kernel(h_ref, kt_ref, v_ref, wq_ref, wo_ref, g_ref, b_ref, o_ref, wq_s, wo_s, *, layer):
    step = pl.program_id(0)
    ln_g, ln_b = _ln_params(g_ref, b_ref, layer, 2)

    @pl.when(step < WCHUNKS)
    def _():
        _stage_chunk(step, wq_ref, wq_s, scale=CA_HEAD_DIM ** -0.5)
        _stage_chunk(step, wo_ref, wo_s)

    nsub = TM // SUB
    head_slices = [slice(hd * CA_HEAD_DIM, (hd + 1) * CA_HEAD_DIM) for hd in range(CA_HEADS)]

    def q_proj(i):
        return _dot(h_ref[i * SUB:(i + 1) * SUB].astype(BF16), wq_s[...]).astype(BF16)

    @pl.when(step >= WCHUNKS)
    def _():
        q = q_proj(0)
        for i in range(nsub):
            rows = slice(i * SUB, (i + 1) * SUB)
            scores = [_dot(q[:, sl], kt_ref[sl, :]) for sl in head_slices]
            if i + 1 < nsub:
                q = q_proj(i + 1)
            heads = []
            for s, sl in zip(scores, head_slices):
                p = jnp.exp(s - jnp.max(s, axis=-1, keepdims=True))
                p = p * (1.0 / jnp.sum(p, axis=-1, keepdims=True))
                heads.append(_dot(p.astype(BF16), v_ref[:, sl]))
            o = jnp.concatenate(heads, axis=-1).astype(BF16)
            y = _dot(o, wo_s[...])
            o_ref[rows] = _layer_norm(ALPHA * h_ref[rows] + y, ln_g, ln_b)


def _cross_attn(h, batch, layer, ktmem, vmem, wq, wo, g, b):
    t = h.shape[0]
    per_batch = t // batch // TM
    seq_of = lambda i: jnp.maximum(i - WCHUNKS, 0) // per_batch
    kt_spec = pl.BlockSpec((D_MODEL, N_MEM), lambda i: (0, seq_of(i)))
    v_spec = pl.BlockSpec((N_MEM, D_MODEL), lambda i: (seq_of(i), 0))
    return pl.pallas_call(
        functools.partial(_ca_kernel, layer=layer),
        grid=(WCHUNKS + t // TM,),
        in_specs=[_token_rows(TM), kt_spec, v_spec, _staged(wq.shape, layer),
                  _staged(wo.shape, layer), _resident(g.shape), _resident(b.shape)],
        out_specs=_token_rows(TM),
        out_shape=jax.ShapeDtypeStruct((t, D_MODEL), F32),
        scratch_shapes=[pltpu.VMEM(wq.shape[1:], BF16), pltpu.VMEM(wo.shape[1:], BF16)],
        compiler_params=pltpu.CompilerParams(
            dimension_semantics=("arbitrary",), vmem_limit_bytes=VMEM_LIMIT_BYTES),
        name="cross_attn_ln",
    )(h, ktmem, vmem, wq, wo, g, b)


def _pool_block_diag(w):
    ng, c, d = w.shape
    out = jnp.zeros((ng * c, ng * d), w.dtype)
    for gi in range(ng):
        out = out.at[gi * c:(gi + 1) * c, gi * d:(gi + 1) * d].set(w[gi])
    return out


def kernel(x, mem, ffn1_gate, ffn1_up, ffn1_down, w_in, conv_w, rec_lb, rec_norm_g, pool_w,
           pool_scale, w_out, ca_q, ca_k, ca_v, ca_o, ffn2_gate, ffn2_up, ffn2_down, ln_g, ln_b):
    batch, seq, _ = x.shape
    h = x.reshape(batch * seq, D_MODEL)
    mem2 = mem.reshape(batch * N_MEM, D_MODEL)
    for l in range(DEPTH):
        h = _ffn(h, l, 0, ffn1_gate, ffn1_up, ffn1_down, ln_g, ln_b)
        h = _mixer(h, batch, l, w_in, conv_w, rec_lb, rec_norm_g,
                   _pool_block_diag(pool_w[l]).astype(BF16), pool_scale, w_out, ln_g, ln_b)
        ktmem, vmem = _kv(mem2, l, ca_k, ca_v)
        h = _cross_attn(h, batch, l, ktmem, vmem, ca_q, ca_o, ln_g, ln_b)
        h = _ffn(h, l, 3, ffn2_gate, ffn2_up, ffn2_down, ln_g, ln_b)
    return h.reshape(batch, seq, D_MODEL)
```

```python
import functools

import jax
import jax.numpy as jnp
from jax import lax
from jax.experimental import pallas as pl
from jax.experimental.pallas import tpu as pltpu

D_MODEL = 1024
DEPTH = 2
N_MEM = 256
CONV_DIM = 256
CONV_WIDTH = 3
REC_HEADS = 4
REC_DIM = 512
REC_HEAD_DIM = 128
REC_FDIM = 512
POOL_DIM = 256
POOL_WINDOWS = (2, 4, 8, 16)
POOL_GROUP = 64
D_IN = 3 * CONV_DIM + 2 * REC_FDIM + 2 * REC_DIM + POOL_DIM
CA_HEADS = 4
CA_HEAD_DIM = 256
D_FF = 2816
ALPHA = (2.0 * DEPTH) ** 0.25
LN_EPS = 1e-5
RMS_EPS = 1e-6

F32 = jnp.float32
BF16 = jnp.bfloat16

VMEM_LIMIT_BYTES = 56 * 1024 * 1024
SUBLANES = 8
TM = 1024
SUB_ROWS = (256, 256, 256, 256)
WCHUNKS = 8
KV_COLS = 512
TS = 256
HALF = TS // 2
PAIR = 2 * REC_HEAD_DIM
CONV_HIST = 8
POOL_HIST = 16
LEVELS = (1, 2, 4, 8, 16, 32, 64, 128)
FILL_COLS = 256
FILL_PLAN_FIRST = (2, 1) + (1, 1, 1, 1, 1, 0, 0, 0) + (0, 0, 0, 0) + (1, 1, 1, 1)
FILL_PLAN_LATER = (0, 0) + (1, 1, 1, 1, 1, 1, 1, 1) + (0, 0, 0, 0) + (1, 1, 1, 1)
MIX_BLOCKS = 4

assert HALF == REC_HEAD_DIM and D_IN % FILL_COLS == 0


def _layer_norm(y, g, b):
    mu = jnp.mean(y, axis=-1, keepdims=True)
    d = y - mu
    var = jnp.mean(d * d, axis=-1, keepdims=True)
    return d * lax.rsqrt(var + LN_EPS) * g + b


def _dot(a, b):
    return jnp.dot(a, b, preferred_element_type=F32)


def _dot_nt(a, b):
    return lax.dot_general(a, b, (((1,), (1,)), ((), ())), preferred_element_type=F32)


def _resident(shape):
    return pl.BlockSpec(shape, lambda *_: (0,) * len(shape), pipeline_mode=pl.Buffered(1))


def _staged(stacked_shape, layer, nchunks=WCHUNKS):
    _, rows, cols = stacked_shape
    return pl.BlockSpec((None, rows // nchunks, cols),
                        lambda i: (layer, jnp.minimum(i, nchunks - 1), 0))


def _ln_params(g_ref, b_ref, layer, idx):
    return g_ref[layer, idx:idx + 1, :], b_ref[layer, idx:idx + 1, :]


def _token_rows(rows):
    return pl.BlockSpec((rows, D_MODEL), lambda i: (jnp.maximum(i - WCHUNKS, 0), 0))


def _sub_blocks():
    edges = [0]
    for rows in SUB_ROWS:
        edges.append(edges[-1] + rows)
    assert edges[-1] == TM
    return [slice(lo, hi) for lo, hi in zip(edges, edges[1:])]


def _stage_chunk(i, src_ref, dst_ref, scale=None):
    rows = src_ref.shape[0]
    w = src_ref[...] if scale is None else src_ref[...] * scale
    dst_ref[pl.ds(pl.multiple_of(i * rows, rows), rows), :] = w.astype(BF16)


def _ffn_kernel(h_ref, wg_ref, wu_ref, wd_ref, g_ref, b_ref, o_ref, wg_s, wu_s, wd_s,
                *, layer, ln_idx):
    step = pl.program_id(0)
    ln_g, ln_b = _ln_params(g_ref, b_ref, layer, ln_idx)

    @pl.when(step < WCHUNKS)
    def _():
        _stage_chunk(step, wg_ref, wg_s)
        _stage_chunk(step, wu_ref, wu_s)
        _stage_chunk(step, wd_ref, wd_s, scale=0.5)

    def up_stage(rows):
        xb = h_ref[rows].astype(BF16)
        gate = _dot(xb, wg_s[...])
        up = _dot(xb, wu_s[...])
        return (gate * jax.nn.sigmoid(gate) * up).astype(BF16)

    def down_stage(rows, act):
        y = _dot(act, wd_s[...])
        o_ref[rows] = _layer_norm(ALPHA * h_ref[rows] + y, ln_g, ln_b)

    @pl.when(step >= WCHUNKS)
    def _():
        blocks = _sub_blocks()
        act = up_stage(blocks[0])
        for prev, rows in zip(blocks, blocks[1:]):
            nxt = up_stage(rows)
            down_stage(prev, act)
            act = nxt
        down_stage(blocks[-1], act)


def _ffn(h, layer, ln_idx, wg, wu, wd, g, b):
    t = h.shape[0]
    return pl.pallas_call(
        functools.partial(_ffn_kernel, layer=layer, ln_idx=ln_idx),
        grid=(WCHUNKS + t // TM,),
        in_specs=[_token_rows(TM), _staged(wg.shape, layer), _staged(wu.shape, layer),
                  _staged(wd.shape, layer), _resident(g.shape), _resident(b.shape)],
        out_specs=_token_rows(TM),
        out_shape=jax.ShapeDtypeStruct((t, D_MODEL), F32),
        scratch_shapes=[pltpu.VMEM(wg.shape[1:], BF16), pltpu.VMEM(wu.shape[1:], BF16),
                        pltpu.VMEM(wd.shape[1:], BF16)],
        compiler_params=pltpu.CompilerParams(
            dimension_semantics=("arbitrary",), vmem_limit_bytes=VMEM_LIMIT_BYTES),
        name="ffn_ln",
    )(h, wg, wu, wd, g, b)


def _mixer_tile(x, z, tile_idx, layer, fill, convw_ref, reclb_ref, normg_ref, poolw_ref,
                poolscale_ref, wout_ref, g_ref, beta_ref, state_ref, uprev_ref, pprev_ref, bcum_ref):
    def proj(lo, width):
        return z[:, lo:lo + width]

    fill()

    c0 = 0
    cb = proj(c0, CONV_DIM)
    cc = proj(c0 + CONV_DIM, CONV_DIM)
    ch = proj(c0 + 2 * CONV_DIM, CONV_DIM)
    u = cc * ch
    uext = jnp.concatenate([uprev_ref[...], u], axis=0)
    u1 = pltpu.roll(uext, 1, axis=0)[CONV_HIST:]
    u2 = pltpu.roll(uext, 2, axis=0)[CONV_HIST:]
    uprev_ref[...] = u[TS - CONV_HIST:]
    cw = convw_ref[layer]
    y_conv = cb * (cw[2:3] * u + cw[1:2] * u1 + cw[0:1] * u2)

    p0 = 3 * CONV_DIM + 2 * REC_FDIM + 2 * REC_DIM
    pu = proj(p0, POOL_DIM)
    pext = jnp.concatenate([pprev_ref[...], pu], axis=0)
    pprev_ref[...] = pu[TS - POOL_HIST:]
    lane = lax.broadcasted_iota(jnp.int32, (TS, POOL_DIM), 1)
    tpos = tile_idx * TS + lax.broadcasted_iota(jnp.int32, (TS, POOL_DIM), 0)
    acc = pext
    wsum = None
    win = None
    span = 1
    for gi, w in enumerate(POOL_WINDOWS):
        while span < w:
            acc = acc + pltpu.roll(acc, span, axis=0)
            span *= 2
        part = acc[POOL_HIST:]
        in_group = lane >= gi * POOL_GROUP
        wsum = part if wsum is None else jnp.where(in_group, part, wsum)
        win = jnp.full((TS, POOL_DIM), w, jnp.int32) if win is None else jnp.where(in_group, w, win)
    count = jnp.minimum(tpos + 1, win).astype(F32)
    pooled = (wsum / count - pu).astype(BF16)
    y_pool = _dot(pooled, poolw_ref[...]) * poolscale_ref[layer:layer + 1, :]
    fill()

    r0 = 3 * CONV_DIM
    rq = proj(r0, REC_FDIM)
    rf = proj(r0 + REC_FDIM, REC_FDIM)
    rv = proj(r0 + 2 * REC_FDIM, REC_DIM)
    rg = proj(r0 + 2 * REC_FDIM + REC_DIM, REC_DIM)

    rl = reclb_ref[...]
    e = jnp.exp(rl - jnp.max(rl, axis=0, keepdims=True))
    sm = e / jnp.sum(e, axis=0, keepdims=True)
    lb = jnp.sum(sm[0:layer + 1], axis=0, keepdims=True) - sm[0:1]

    f = lb + (1.0 - lb) * jax.nn.sigmoid(rf)
    log2f = jnp.log2(f)
    k = 1.0 - f

    row2 = lax.broadcasted_iota(jnp.int32, (TS, TS), 0)
    col2 = lax.broadcasted_iota(jnp.int32, (TS, TS), 1)
    tri = jnp.where(row2 >= col2, 1.0, 0.0).astype(BF16)
    l_hi = log2f.astype(BF16)
    l_lo = (log2f - l_hi.astype(F32)).astype(BF16)
    bcum = _dot(tri, l_hi) + _dot(tri, l_lo)
    bcum_ref[...] = bcum
    b_last = bcum_ref[TS - 1:TS, :]

    q_in = (rq * jnp.exp2(bcum)).astype(BF16)
    k_out = (k * jnp.exp2(b_last - bcum)).astype(BF16)
    decay_all = jnp.exp2(b_last)

    rbit = lax.broadcasted_iota(jnp.int32, (TS, REC_FDIM), 0)
    b3 = bcum.reshape(TS // SUBLANES, SUBLANES, REC_FDIM)
    sub3 = lax.broadcasted_iota(jnp.int32, b3.shape, 1)

    def level_operand(half):
        if half >= SUBLANES:
            parts = []
            for gidx in range(TS // (2 * half)):
                lo = 2 * half * gidx
                ref = bcum_ref[lo + half - 1:lo + half, :]
                parts.append(k[lo:lo + half] * jnp.exp2(ref - bcum[lo:lo + half]))
                parts.append(rq[lo + half:lo + 2 * half]
                             * jnp.exp2(bcum[lo + half:lo + 2 * half] - ref))
            return jnp.concatenate(parts, axis=0)
        if half == 1:
            ref = jnp.where((sub3 & 1) == 1, pltpu.roll(b3, 1, axis=1), b3)
        elif half == 2:
            m = sub3 & 3
            ref = jnp.where(m == 0, pltpu.roll(b3, SUBLANES - 1, axis=1),
                            jnp.where(m == 1, b3,
                                      jnp.where(m == 2, pltpu.roll(b3, 1, axis=1),
                                                pltpu.roll(b3, 2, axis=1))))
        else:
            ref = jnp.concatenate(
                [jnp.broadcast_to(bcum_ref[SUBLANES * gidx + half - 1:SUBLANES * gidx + half, :],
                                  (1, SUBLANES, REC_FDIM))
                 for gidx in range(TS // SUBLANES)], axis=0)
        gfac = jnp.exp2(-jnp.abs(b3 - ref)).reshape(TS, REC_FDIM)
        return jnp.where((rbit & half) != 0, rq, k) * gfac

    w_lvl = {}
    for half in LEVELS:
        fill()
        w_lvl[half] = level_operand(half).astype(BF16)

    trow = lax.broadcasted_iota(jnp.int32, (HALF, PAIR), 0)
    tcol = lax.broadcasted_iota(jnp.int32, (HALF, PAIR), 1) & (HALF - 1)
    xor = trow ^ tcol
    causal = trow > tcol
    lvl_mask = {half: xor >= half for half in LEVELS[:-1]}
    zero_pair = jnp.zeros((HALF, PAIR), F32)
    zero_blk = jnp.zeros((HALF, HALF), F32)
    zero_key = jnp.zeros((HALF, REC_HEAD_DIM), BF16)

    def pair_nt(lhs, keys):
        rhs = jnp.concatenate(
            [jnp.concatenate([keys[:, :REC_HEAD_DIM], zero_key], axis=1),
             jnp.concatenate([zero_key, keys[:, REC_HEAD_DIM:]], axis=1)], axis=0)
        return _dot_nt(lhs, rhs)

    attn = []
    for pr in range(REC_HEADS // 2):
        lanes = slice(pr * PAIR, (pr + 1) * PAIR)
        diag = []
        for d in range(2):
            rows = slice(d * HALF, (d + 1) * HALF)
            blk = zero_pair
            for half in LEVELS[:-1]:
                wl = w_lvl[half][rows, lanes]
                blk = jnp.where(lvl_mask[half], pair_nt(wl, wl), blk)
            diag.append(jnp.where(causal, blk, 0.0))
            fill()
        wl = w_lvl[HALF]
        cross = pair_nt(wl[HALF:, lanes], wl[:HALF, lanes])
        for j in range(2):
            cs = slice(j * HALF, (j + 1) * HALF)
            attn.append(jnp.concatenate(
                [jnp.concatenate([diag[0][:, cs], zero_blk], axis=1),
                 jnp.concatenate([cross[:, cs], diag[1][:, cs]], axis=1)], axis=0).astype(BF16))

    sg = jax.nn.sigmoid(rg)
    ng = normg_ref[layer:layer + 1, :]
    y_rec = []
    for hd in range(REC_HEADS):
        fill()
        sl = slice(hd * REC_HEAD_DIM, (hd + 1) * REC_HEAD_DIM)
        qh = rq[:, sl]
        kh = k[:, sl]
        vh = rv[:, sl]
        st = state_ref[hd]
        o = (_dot(attn[hd], vh.astype(BF16))
             + jnp.sum(qh * kh, axis=-1, keepdims=True) * vh
             + _dot_nt(q_in[:, sl], st.astype(BF16)))
        state_ref[hd] = st * decay_all[:, sl] + _dot(vh.T.astype(BF16), k_out[:, sl])
        o = o * lax.rsqrt(jnp.mean(o * o, axis=-1, keepdims=True) + RMS_EPS)
        y_rec.append(o * ng[:, sl] * sg[:, sl])

    mix = jnp.concatenate([y_conv] + y_rec + [y_pool], axis=-1).astype(BF16)
    y = _dot(mix, wout_ref[...])
    return _layer_norm(ALPHA * x + y, *_ln_params(g_ref, beta_ref, layer, 1))


def _mixer_kernel(h_ref, hnext_ref, win_ref, convw_ref, reclb_ref, normg_ref, poolw_ref,
                  poolscale_ref, wout_ref, g_ref, beta_ref, o_ref,
                  win_s, wout_s, z0_ref, z1_ref, state_ref, uprev_ref, pprev_ref, bcum_ref,
                  *, layer, steps_per_seq):
    step = pl.program_id(0)

    @pl.when(step < WCHUNKS)
    def _():
        _stage_chunk(step, win_ref, win_s)
        _stage_chunk(step, wout_ref, wout_s)

    @pl.when(step >= WCHUNKS)
    def _():
        s_idx = lax.rem(step - WCHUNKS, steps_per_seq)

        @pl.when(s_idx == 0)
        def _():
            state_ref[...] = jnp.zeros_like(state_ref)
            uprev_ref[...] = jnp.zeros_like(uprev_ref)
            pprev_ref[...] = jnp.zeros_like(pprev_ref)

        @pl.when(step == WCHUNKS)
        def _():
            z0_ref[...] = _dot(h_ref[0:TS].astype(BF16), win_s[...])

        tile_refs = (convw_ref, reclb_ref, normg_ref, poolw_ref, poolscale_ref, wout_s, g_ref,
                     beta_ref, state_ref, uprev_ref, pprev_ref, bcum_ref)

        def projector(xb, dst_ref, plan):
            todo = list(range(0, D_IN, FILL_COLS))
            counts = list(plan)

            def fill(flush=False):
                for _ in range(len(todo) if flush else counts.pop(0)):
                    lo = todo.pop(0)
                    dst_ref[:, lo:lo + FILL_COLS] = _dot(xb, win_s[:, lo:lo + FILL_COLS])

            return fill

        z_refs = (z0_ref, z1_ref)
        for j in range(MIX_BLOCKS):
            rows = slice(j * TS, (j + 1) * TS)
            if j + 1 < MIX_BLOCKS:
                upcoming = h_ref[(j + 1) * TS:(j + 2) * TS]
            else:
                upcoming = hnext_ref[...]
            fill = projector(upcoming.astype(BF16), z_refs[(j + 1) % 2],
                             FILL_PLAN_FIRST if j == 0 else FILL_PLAN_LATER)
            o_ref[rows] = _mixer_tile(h_ref[rows], z_refs[j % 2][...], MIX_BLOCKS * s_idx + j,
                                      layer, fill, *tile_refs)
            fill(flush=True)


def _mixer(h, batch, layer, w_in, conv_w, rec_lb, norm_g, pool_w, pool_scale, w_out, g, b):
    t = h.shape[0]
    steps_per_seq = t // batch // (MIX_BLOCKS * TS)
    last_tile = t // TS - 1
    nxt = pl.BlockSpec(
        (TS, D_MODEL),
        lambda i: (jnp.minimum(MIX_BLOCKS * (jnp.maximum(i - WCHUNKS, 0) + 1), last_tile), 0))
    small = (conv_w, rec_lb, norm_g, pool_w, pool_scale)
    return pl.pallas_call(
        functools.partial(_mixer_kernel, layer=layer, steps_per_seq=steps_per_seq),
        grid=(WCHUNKS + batch * steps_per_seq,),
        in_specs=([_token_rows(MIX_BLOCKS * TS), nxt, _staged(w_in.shape, layer)]
                  + [_resident(p.shape) for p in small]
                  + [_staged(w_out.shape, layer), _resident(g.shape), _resident(b.shape)]),
        out_specs=_token_rows(MIX_BLOCKS * TS),
        out_shape=jax.ShapeDtypeStruct((t, D_MODEL), F32),
        scratch_shapes=[
            pltpu.VMEM(w_in.shape[1:], BF16),
            pltpu.VMEM(w_out.shape[1:], BF16),
            pltpu.VMEM((TS, D_IN), F32),
            pltpu.VMEM((TS, D_IN), F32),
            pltpu.VMEM((REC_HEADS, REC_HEAD_DIM, REC_HEAD_DIM), F32),
            pltpu.VMEM((CONV_HIST, CONV_DIM), F32),
            pltpu.VMEM((POOL_HIST, POOL_DIM), F32),
            pltpu.VMEM((TS, REC_FDIM), F32),
        ],
        compiler_params=pltpu.CompilerParams(
            dimension_semantics=("arbitrary",), vmem_limit_bytes=VMEM_LIMIT_BYTES),
        name="mixer_ln",
    )(h, h, w_in, *small, w_out, g, b)


def _kv_kernel(mem_ref, wk_ref, wv_ref, kt_ref, v_ref):
    mb = mem_ref[...].astype(BF16)
    kt_ref[...] = _dot(mb, wk_ref[...].astype(BF16)).T.astype(BF16)
    v_ref[...] = _dot(mb, wv_ref[...].astype(BF16)).astype(BF16)


def _kv(mem, layer, wk, wv):
    m = mem.shape[0]
    wspec = pl.BlockSpec((None, D_MODEL, KV_COLS), lambda j: (layer, 0, j))
    return pl.pallas_call(
        _kv_kernel,
        grid=(D_MODEL // KV_COLS,),
        in_specs=[_resident(mem.shape), wspec, wspec],
        out_specs=[pl.BlockSpec((KV_COLS, m), lambda j: (j, 0)),
                   pl.BlockSpec((m, KV_COLS), lambda j: (0, j))],
        out_shape=[jax.ShapeDtypeStruct((D_MODEL, m), BF16),
                   jax.ShapeDtypeStruct((m, D_MODEL), BF16)],
        compiler_params=pltpu.CompilerParams(
            dimension_semantics=("parallel",), vmem_limit_bytes=VMEM_LIMIT_BYTES),
        name="mem_kv",
    )(mem, wk, wv)


def _ca_kernel(h_ref, kt_ref, v_ref, wq_ref, wo_ref, g_ref, b_ref, o_ref, wq_s, wo_s, *, layer):
    step = pl.program_id(0)
    ln_g, ln_b = _ln_params(g_ref, b_ref, layer, 2)

    @pl.when(step < WCHUNKS)
    def _():
        _stage_chunk(step, wq_ref, wq_s, scale=CA_HEAD_DIM ** -0.5)
        _stage_chunk(step, wo_ref, wo_s)

    head_slices = [slice(hd * CA_HEAD_DIM, (hd + 1) * CA_HEAD_DIM) for hd in range(CA_HEADS)]

    def q_proj(rows):
        return _dot(h_ref[rows].astype(BF16), wq_s[...]).astype(BF16)

    def score(q):
        return [_dot(q[:, sl], kt_ref[sl, :]) for sl in head_slices]

    @pl.when(step >= WCHUNKS)
    def _():
        blocks = _sub_blocks()
        scores = score(q_proj(blocks[0]))
        q_next = q_proj(blocks[1]) if len(blocks) > 1 else None
        for i, rows in enumerate(blocks):
            heads = []
            for s, sl in zip(scores, head_slices):
                p = jnp.exp(s - jnp.max(s, axis=-1, keepdims=True))
                p = p * (1.0 / jnp.sum(p, axis=-1, keepdims=True))
                heads.append(_dot(p.astype(BF16), v_ref[:, sl]))
            if i + 1 < len(blocks):
                scores = score(q_next)
            if i + 2 < len(blocks):
                q_next = q_proj(blocks[i + 2])
            o = jnp.concatenate(heads, axis=-1).astype(BF16)
            y = _dot(o, wo_s[...])
            o_ref[rows] = _layer_norm(ALPHA * h_ref[rows] + y, ln_g, ln_b)


def _cross_attn(h, batch, layer, ktmem, vmem, wq, wo, g, b):
    t = h.shape[0]
    per_batch = t // batch // TM
    seq_of = lambda i: jnp.maximum(i - WCHUNKS, 0) // per_batch
    kt_spec = pl.BlockSpec((D_MODEL, N_MEM), lambda i: (0, seq_of(i)))
    v_spec = pl.BlockSpec((N_MEM, D_MODEL), lambda i: (seq_of(i), 0))
    return pl.pallas_call(
        functools.partial(_ca_kernel, layer=layer),
        grid=(WCHUNKS + t // TM,),
        in_specs=[_token_rows(TM), kt_spec, v_spec, _staged(wq.shape, layer),
                  _staged(wo.shape, layer), _resident(g.shape), _resident(b.shape)],
        out_specs=_token_rows(TM),
        out_shape=jax.ShapeDtypeStruct((t, D_MODEL), F32),
        scratch_shapes=[pltpu.VMEM(wq.shape[1:], BF16), pltpu.VMEM(wo.shape[1:], BF16)],
        compiler_params=pltpu.CompilerParams(
            dimension_semantics=("arbitrary",), vmem_limit_bytes=VMEM_LIMIT_BYTES),
        name="cross_attn_ln",
    )(h, ktmem, vmem, wq, wo, g, b)


def _pool_block_diag(w):
    ng, c, d = w.shape
    out = jnp.zeros((ng * c, ng * d), w.dtype)
    for gi in range(ng):
        out = out.at[gi * c:(gi + 1) * c, gi * d:(gi + 1) * d].set(w[gi])
    return out


def kernel(x, mem, ffn1_gate, ffn1_up, ffn1_down, w_in, conv_w, rec_lb, rec_norm_g, pool_w,
           pool_scale, w_out, ca_q, ca_k, ca_v, ca_o, ffn2_gate, ffn2_up, ffn2_down, ln_g, ln_b):
    batch, seq, _ = x.shape
    h = x.reshape(batch * seq, D_MODEL)
    mem2 = mem.reshape(batch * N_MEM, D_MODEL)
    for l in range(DEPTH):
        h = _ffn(h, l, 0, ffn1_gate, ffn1_up, ffn1_down, ln_g, ln_b)
        h = _mixer(h, batch, l, w_in, conv_w, rec_lb, rec_norm_g,
                   _pool_block_diag(pool_w[l]).astype(BF16), pool_scale, w_out, ln_g, ln_b)
        ktmem, vmem = _kv(mem2, l, ca_k, ca_v)
        h = _cross_attn(h, batch, l, ktmem, vmem, ca_q, ca_o, ln_g, ln_b)
        h = _ffn(h, l, 3, ffn2_gate, ffn2_up, ffn2_down, ln_g, ln_b)
    return h.reshape(batch, seq, D_MODEL)
```

```python
import functools

import jax
import jax.numpy as jnp
from jax import lax
from jax.experimental import pallas as pl
from jax.experimental.pallas import tpu as pltpu

D_MODEL = 1024
DEPTH = 2
N_MEM = 256
CONV_DIM = 256
CONV_WIDTH = 3
REC_HEADS = 4
REC_DIM = 512
REC_HEAD_DIM = 128
REC_FDIM = 512
POOL_DIM = 256
POOL_WINDOWS = (2, 4, 8, 16)
POOL_GROUP = 64
D_IN = 3 * CONV_DIM + 2 * REC_FDIM + 2 * REC_DIM + POOL_DIM
CA_HEADS = 4
CA_HEAD_DIM = 256
D_FF = 2816
ALPHA = (2.0 * DEPTH) ** 0.25
LN_EPS = 1e-5
RMS_EPS = 1e-6

F32 = jnp.float32
BF16 = jnp.bfloat16

VMEM_LIMIT_BYTES = 56 * 1024 * 1024
SUBLANES = 8
TM = 1024
SUB_ROWS = (256, 256, 256, 256)
WCHUNKS = 8
KV_COLS = 512
TS = 256
HALF = TS // 2
PAIR = 2 * REC_HEAD_DIM
CONV_HIST = 8
POOL_HIST = 16
LEVELS = (1, 2, 4, 8, 16, 32, 64, 128)
FILL_COLS = 256
FILL_PLAN_FIRST = (2, 1) + (1, 1, 1, 1, 1, 0, 0, 0) + (0, 0, 0, 0) + (1, 1, 1, 1)
FILL_PLAN_LATER = (0, 0) + (1, 1, 1, 1, 1, 1, 1, 1) + (0, 0, 0, 0) + (1, 1, 1, 1)
MIX_BLOCKS = 2

assert HALF == REC_HEAD_DIM and D_IN % FILL_COLS == 0


def _layer_norm(y, g, b):
    mu = jnp.mean(y, axis=-1, keepdims=True)
    d = y - mu
    var = jnp.mean(d * d, axis=-1, keepdims=True)
    return d * lax.rsqrt(var + LN_EPS) * g + b


def _dot(a, b):
    return jnp.dot(a, b, preferred_element_type=F32)


def _dot_nt(a, b):
    return lax.dot_general(a, b, (((1,), (1,)), ((), ())), preferred_element_type=F32)


def _resident(shape):
    return pl.BlockSpec(shape, lambda *_: (0,) * len(shape), pipeline_mode=pl.Buffered(1))


def _staged(stacked_shape, layer, nchunks=WCHUNKS):
    _, rows, cols = stacked_shape
    return pl.BlockSpec((None, rows // nchunks, cols),
                        lambda i: (layer, jnp.minimum(i, nchunks - 1), 0))


def _ln_params(g_ref, b_ref, layer, idx):
    return g_ref[layer, idx:idx + 1, :], b_ref[layer, idx:idx + 1, :]


def _token_rows(rows):
    return pl.BlockSpec((rows, D_MODEL), lambda i: (jnp.maximum(i - WCHUNKS, 0), 0))


def _sub_blocks():
    edges = [0]
    for rows in SUB_ROWS:
        edges.append(edges[-1] + rows)
    assert edges[-1] == TM
    return [slice(lo, hi) for lo, hi in zip(edges, edges[1:])]


def _stage_chunk(i, src_ref, dst_ref, scale=None):
    rows = src_ref.shape[0]
    w = src_ref[...] if scale is None else src_ref[...] * scale
    dst_ref[pl.ds(pl.multiple_of(i * rows, rows), rows), :] = w.astype(BF16)


def _ffn_kernel(h_ref, wg_ref, wu_ref, wd_ref, g_ref, b_ref, o_ref, wg_s, wu_s, wd_s,
                *, layer, ln_idx):
    step = pl.program_id(0)
    ln_g, ln_b = _ln_params(g_ref, b_ref, layer, ln_idx)

    @pl.when(step < WCHUNKS)
    def _():
        _stage_chunk(step, wg_ref, wg_s)
        _stage_chunk(step, wu_ref, wu_s)
        _stage_chunk(step, wd_ref, wd_s, scale=0.5)

    def up_stage(rows):
        xb = h_ref[rows].astype(BF16)
        gate = _dot(xb, wg_s[...])
        up = _dot(xb, wu_s[...])
        return (gate * jax.nn.sigmoid(gate) * up).astype(BF16)

    def down_stage(rows, act):
        y = _dot(act, wd_s[...])
        o_ref[rows] = _layer_norm(ALPHA * h_ref[rows] + y, ln_g, ln_b)

    @pl.when(step >= WCHUNKS)
    def _():
        blocks = _sub_blocks()
        act = up_stage(blocks[0])
        for prev, rows in zip(blocks, blocks[1:]):
            nxt = up_stage(rows)
            down_stage(prev, act)
            act = nxt
        down_stage(blocks[-1], act)


def _ffn(h, layer, ln_idx, wg, wu, wd, g, b):
    t = h.shape[0]
    return pl.pallas_call(
        functools.partial(_ffn_kernel, layer=layer, ln_idx=ln_idx),
        grid=(WCHUNKS + t // TM,),
        in_specs=[_token_rows(TM), _staged(wg.shape, layer), _staged(wu.shape, layer),
                  _staged(wd.shape, layer), _resident(g.shape), _resident(b.shape)],
        out_specs=_token_rows(TM),
        out_shape=jax.ShapeDtypeStruct((t, D_MODEL), F32),
        scratch_shapes=[pltpu.VMEM(wg.shape[1:], BF16), pltpu.VMEM(wu.shape[1:], BF16),
                        pltpu.VMEM(wd.shape[1:], BF16)],
        compiler_params=pltpu.CompilerParams(
            dimension_semantics=("arbitrary",), vmem_limit_bytes=VMEM_LIMIT_BYTES),
        name="ffn_ln",
    )(h, wg, wu, wd, g, b)


def _mixer_tile(x, z, tile_idx, layer, fill, convw_ref, reclb_ref, normg_ref, poolw_ref,
                poolscale_ref, wout_ref, g_ref, beta_ref, state_ref, uprev_ref, pprev_ref, bcum_ref):
    def proj(lo, width):
        return z[:, lo:lo + width]

    fill()

    c0 = 0
    cb = proj(c0, CONV_DIM)
    cc = proj(c0 + CONV_DIM, CONV_DIM)
    ch = proj(c0 + 2 * CONV_DIM, CONV_DIM)
    u = cc * ch
    uext = jnp.concatenate([uprev_ref[...], u], axis=0)
    u1 = pltpu.roll(uext, 1, axis=0)[CONV_HIST:]
    u2 = pltpu.roll(uext, 2, axis=0)[CONV_HIST:]
    uprev_ref[...] = u[TS - CONV_HIST:]
    cw = convw_ref[layer]
    y_conv = cb * (cw[2:3] * u + cw[1:2] * u1 + cw[0:1] * u2)

    p0 = 3 * CONV_DIM + 2 * REC_FDIM + 2 * REC_DIM
    pu = proj(p0, POOL_DIM)
    pext = jnp.concatenate([pprev_ref[...], pu], axis=0)
    pprev_ref[...] = pu[TS - POOL_HIST:]
    lane = lax.broadcasted_iota(jnp.int32, (TS, POOL_DIM), 1)
    tpos = tile_idx * TS + lax.broadcasted_iota(jnp.int32, (TS, POOL_DIM), 0)
    acc = pext
    wsum = None
    win = None
    span = 1
    for gi, w in enumerate(POOL_WINDOWS):
        while span < w:
            acc = acc + pltpu.roll(acc, span, axis=0)
            span *= 2
        part = acc[POOL_HIST:]
        in_group = lane >= gi * POOL_GROUP
        wsum = part if wsum is None else jnp.where(in_group, part, wsum)
        win = jnp.full((TS, POOL_DIM), w, jnp.int32) if win is None else jnp.where(in_group, w, win)
    count = jnp.minimum(tpos + 1, win).astype(F32)
    pooled = (wsum / count - pu).astype(BF16)
    y_pool = _dot(pooled, poolw_ref[...]) * poolscale_ref[layer:layer + 1, :]
    fill()

    r0 = 3 * CONV_DIM
    rq = proj(r0, REC_FDIM)
    rf = proj(r0 + REC_FDIM, REC_FDIM)
    rv = proj(r0 + 2 * REC_FDIM, REC_DIM)
    rg = proj(r0 + 2 * REC_FDIM + REC_DIM, REC_DIM)

    rl = reclb_ref[...]
    e = jnp.exp(rl - jnp.max(rl, axis=0, keepdims=True))
    sm = e / jnp.sum(e, axis=0, keepdims=True)
    lb = jnp.sum(sm[0:layer + 1], axis=0, keepdims=True) - sm[0:1]

    f = lb + (1.0 - lb) * jax.nn.sigmoid(rf)
    log2f = jnp.log2(f)
    k = 1.0 - f

    row2 = lax.broadcasted_iota(jnp.int32, (TS, TS), 0)
    col2 = lax.broadcasted_iota(jnp.int32, (TS, TS), 1)
    tri = jnp.where(row2 >= col2, 1.0, 0.0).astype(BF16)
    l_hi = log2f.astype(BF16)
    l_lo = (log2f - l_hi.astype(F32)).astype(BF16)
    bcum = _dot(tri, l_hi) + _dot(tri, l_lo)
    bcum_ref[...] = bcum
    b_last = bcum_ref[TS - 1:TS, :]

    q_in = (rq * jnp.exp2(bcum)).astype(BF16)
    k_out = (k * jnp.exp2(b_last - bcum)).astype(BF16)
    decay_all = jnp.exp2(b_last)

    rbit = lax.broadcasted_iota(jnp.int32, (TS, REC_FDIM), 0)
    b3 = bcum.reshape(TS // SUBLANES, SUBLANES, REC_FDIM)
    sub3 = lax.broadcasted_iota(jnp.int32, b3.shape, 1)

    def level_operand(half):
        if half >= SUBLANES:
            parts = []
            for gidx in range(TS // (2 * half)):
                lo = 2 * half * gidx
                ref = bcum_ref[lo + half - 1:lo + half, :]
                parts.append(k[lo:lo + half] * jnp.exp2(ref - bcum[lo:lo + half]))
                parts.append(rq[lo + half:lo + 2 * half]
                             * jnp.exp2(bcum[lo + half:lo + 2 * half] - ref))
            return jnp.concatenate(parts, axis=0)
        if half == 1:
            ref = jnp.where((sub3 & 1) == 1, pltpu.roll(b3, 1, axis=1), b3)
        elif half == 2:
            m = sub3 & 3
            ref = jnp.where(m == 0, pltpu.roll(b3, SUBLANES - 1, axis=1),
                            jnp.where(m == 1, b3,
                                      jnp.where(m == 2, pltpu.roll(b3, 1, axis=1),
                                                pltpu.roll(b3, 2, axis=1))))
        else:
            ref = jnp.concatenate(
                [jnp.broadcast_to(bcum_ref[SUBLANES * gidx + half - 1:SUBLANES * gidx + half, :],
                                  (1, SUBLANES, REC_FDIM))
                 for gidx in range(TS // SUBLANES)], axis=0)
        gfac = jnp.exp2(-jnp.abs(b3 - ref)).reshape(TS, REC_FDIM)
        return jnp.where((rbit & half) != 0, rq, k) * gfac

    w_lvl = {}
    for half in LEVELS:
        fill()
        w_lvl[half] = level_operand(half).astype(BF16)

    trow = lax.broadcasted_iota(jnp.int32, (HALF, PAIR), 0)
    tcol = lax.broadcasted_iota(jnp.int32, (HALF, PAIR), 1) & (HALF - 1)
    xor = trow ^ tcol
    causal = trow > tcol
    lvl_mask = {half: xor >= half for half in LEVELS[:-1]}
    zero_pair = jnp.zeros((HALF, PAIR), F32)
    zero_blk = jnp.zeros((HALF, HALF), F32)
    zero_key = jnp.zeros((HALF, REC_HEAD_DIM), BF16)

    def pair_nt(lhs, keys):
        rhs = jnp.concatenate(
            [jnp.concatenate([keys[:, :REC_HEAD_DIM], zero_key], axis=1),
             jnp.concatenate([zero_key, keys[:, REC_HEAD_DIM:]], axis=1)], axis=0)
        return _dot_nt(lhs, rhs)

    attn = []
    for pr in range(REC_HEADS // 2):
        lanes = slice(pr * PAIR, (pr + 1) * PAIR)
        diag = []
        for d in range(2):
            rows = slice(d * HALF, (d + 1) * HALF)
            blk = zero_pair
            for half in LEVELS[:-1]:
                wl = w_lvl[half][rows, lanes]
                blk = jnp.where(lvl_mask[half], pair_nt(wl, wl), blk)
            diag.append(jnp.where(causal, blk, 0.0))
            fill()
        wl = w_lvl[HALF]
        cross = pair_nt(wl[HALF:, lanes], wl[:HALF, lanes])
        for j in range(2):
            cs = slice(j * HALF, (j + 1) * HALF)
            attn.append(jnp.concatenate(
                [jnp.concatenate([diag[0][:, cs], zero_blk], axis=1),
                 jnp.concatenate([cross[:, cs], diag[1][:, cs]], axis=1)], axis=0).astype(BF16))

    sg = jax.nn.sigmoid(rg)
    ng = normg_ref[layer:layer + 1, :]
    y_rec = []
    for hd in range(REC_HEADS):
        fill()
        sl = slice(hd * REC_HEAD_DIM, (hd + 1) * REC_HEAD_DIM)
        qh = rq[:, sl]
        kh = k[:, sl]
        vh = rv[:, sl]
        st = state_ref[hd]
        o = (_dot(attn[hd], vh.astype(BF16))
             + jnp.sum(qh * kh, axis=-1, keepdims=True) * vh
             + _dot_nt(q_in[:, sl], st.astype(BF16)))
        state_ref[hd] = st * decay_all[:, sl] + _dot(vh.T.astype(BF16), k_out[:, sl])
        o = o * lax.rsqrt(jnp.mean(o * o, axis=-1, keepdims=True) + RMS_EPS)
        y_rec.append(o * ng[:, sl] * sg[:, sl])

    mix = jnp.concatenate([y_conv] + y_rec + [y_pool], axis=-1).astype(BF16)
    y = _dot(mix, wout_ref[...])
    return _layer_norm(ALPHA * x + y, *_ln_params(g_ref, beta_ref, layer, 1))


def _mixer_kernel(h_ref, hnext_ref, win_ref, convw_ref, reclb_ref, normg_ref, poolw_ref,
                  poolscale_ref, wout_ref, g_ref, beta_ref, o_ref,
                  win_s, wout_s, z0_ref, z1_ref, state_ref, uprev_ref, pprev_ref, bcum_ref,
                  *, layer, steps_per_seq):
    step = pl.program_id(0)

    @pl.when(step < WCHUNKS)
    def _():
        _stage_chunk(step, win_ref, win_s)
        _stage_chunk(step, wout_ref, wout_s)

    @pl.when(step >= WCHUNKS)
    def _():
        s_idx = lax.rem(step - WCHUNKS, steps_per_seq)

        @pl.when(s_idx == 0)
        def _():
            state_ref[...] = jnp.zeros_like(state_ref)
            uprev_ref[...] = jnp.zeros_like(uprev_ref)
            pprev_ref[...] = jnp.zeros_like(pprev_ref)

        @pl.when(step == WCHUNKS)
        def _():
            z0_ref[...] = _dot(h_ref[0:TS].astype(BF16), win_s[...])

        tile_refs = (convw_ref, reclb_ref, normg_ref, poolw_ref, poolscale_ref, wout_s, g_ref,
                     beta_ref, state_ref, uprev_ref, pprev_ref, bcum_ref)

        def projector(xb, dst_ref, plan):
            todo = list(range(0, D_IN, FILL_COLS))
            counts = list(plan)

            def fill(flush=False):
                for _ in range(len(todo) if flush else counts.pop(0)):
                    lo = todo.pop(0)
                    dst_ref[:, lo:lo + FILL_COLS] = _dot(xb, win_s[:, lo:lo + FILL_COLS])

            return fill

        z_refs = (z0_ref, z1_ref)
        for j in range(MIX_BLOCKS):
            rows = slice(j * TS, (j + 1) * TS)
            if j + 1 < MIX_BLOCKS:
                upcoming = h_ref[(j + 1) * TS:(j + 2) * TS]
            else:
                upcoming = hnext_ref[...]
            fill = projector(upcoming.astype(BF16), z_refs[(j + 1) % 2],
                             FILL_PLAN_FIRST if j == 0 else FILL_PLAN_LATER)
            o_ref[rows] = _mixer_tile(h_ref[rows], z_refs[j % 2][...], MIX_BLOCKS * s_idx + j,
                                      layer, fill, *tile_refs)
            fill(flush=True)


def _mixer(h, batch, layer, w_in, conv_w, rec_lb, norm_g, pool_w, pool_scale, w_out, g, b):
    t = h.shape[0]
    steps_per_seq = t // batch // (MIX_BLOCKS * TS)
    last_tile = t // TS - 1
    nxt = pl.BlockSpec(
        (TS, D_MODEL),
        lambda i: (jnp.minimum(MIX_BLOCKS * (jnp.maximum(i - WCHUNKS, 0) + 1), last_tile), 0))
    small = (conv_w, rec_lb, norm_g, pool_w, pool_scale)
    return pl.pallas_call(
        functools.partial(_mixer_kernel, layer=layer, steps_per_seq=steps_per_seq),
        grid=(WCHUNKS + batch * steps_per_seq,),
        in_specs=([_token_rows(MIX_BLOCKS * TS), nxt, _staged(w_in.shape, layer)]
                  + [_resident(p.shape) for p in small]
                  + [_staged(w_out.shape, layer), _resident(g.shape), _resident(b.shape)]),
        out_specs=_token_rows(MIX_BLOCKS * TS),
        out_shape=jax.ShapeDtypeStruct((t, D_MODEL), F32),
        scratch_shapes=[
            pltpu.VMEM(w_in.shape[1:], BF16),
            pltpu.VMEM(w_out.shape[1:], BF16),
            pltpu.VMEM((TS, D_IN), F32),
            pltpu.VMEM((TS, D_IN), F32),
            pltpu.VMEM((REC_HEADS, REC_HEAD_DIM, REC_HEAD_DIM), F32),
            pltpu.VMEM((CONV_HIST, CONV_DIM), F32),
            pltpu.VMEM((POOL_HIST, POOL_DIM), F32),
            pltpu.VMEM((TS, REC_FDIM), F32),
        ],
        compiler_params=pltpu.CompilerParams(
            dimension_semantics=("arbitrary",), vmem_limit_bytes=VMEM_LIMIT_BYTES),
        name="mixer_ln",
    )(h, h, w_in, *small, w_out, g, b)


def _kv_kernel(mem_ref, wk_ref, wv_ref, kt_ref, v_ref):
    mb = mem_ref[...].astype(BF16)
    kt_ref[...] = _dot(mb, wk_ref[...].astype(BF16)).T.astype(BF16)
    v_ref[...] = _dot(mb, wv_ref[...].astype(BF16)).astype(BF16)


def _kv(mem, layer, wk, wv):
    m = mem.shape[0]
    wspec = pl.BlockSpec((None, D_MODEL, KV_COLS), lambda j: (layer, 0, j))
    return pl.pallas_call(
        _kv_kernel,
        grid=(D_MODEL // KV_COLS,),
        in_specs=[_resident(mem.shape), wspec, wspec],
        out_specs=[pl.BlockSpec((KV_COLS, m), lambda j: (j, 0)),
                   pl.BlockSpec((m, KV_COLS), lambda j: (0, j))],
        out_shape=[jax.ShapeDtypeStruct((D_MODEL, m), BF16),
                   jax.ShapeDtypeStruct((m, D_MODEL), BF16)],
        compiler_params=pltpu.CompilerParams(
            dimension_semantics=("parallel",), vmem_limit_bytes=VMEM_LIMIT_BYTES),
        name="mem_kv",
    )(mem, wk, wv)


def _ca_kernel(h_ref, kt_ref, v_ref, wq_ref, wo_ref, g_ref, b_ref, o_ref, wq_s, wo_s, *, layer):
    step = pl.program_id(0)
    ln_g, ln_b = _ln_params(g_ref, b_ref, layer, 2)

    @pl.when(step < WCHUNKS)
    def _():
        _stage_chunk(step, wq_ref, wq_s, scale=CA_HEAD_DIM ** -0.5)
        _stage_chunk(step, wo_ref, wo_s)

    head_slices = [slice(hd * CA_HEAD_DIM, (hd + 1) * CA_HEAD_DIM) for hd in range(CA_HEADS)]

    def q_proj(rows):
        return _dot(h_ref[rows].astype(BF16), wq_s[...]).astype(BF16)

    def score(q):
        return [_dot(q[:, sl], kt_ref[sl, :]) for sl in head_slices]

    @pl.when(step >= WCHUNKS)
    def _():
        blocks = _sub_blocks()
        scores = score(q_proj(blocks[0]))
        q_next = q_proj(blocks[1]) if len(blocks) > 1 else None
        for i, rows in enumerate(blocks):
            heads = []
            for s, sl in zip(scores, head_slices):
                p = jnp.exp(s - jnp.max(s, axis=-1, keepdims=True))
                p = p * (1.0 / jnp.sum(p, axis=-1, keepdims=True))
                heads.append(_dot(p.astype(BF16), v_ref[:, sl]))
            if i + 1 < len(blocks):
                scores = score(q_next)
            if i + 2 < len(blocks):
                q_next = q_proj(blocks[i + 2])
            o = jnp.concatenate(heads, axis=-1).astype(BF16)
            y = _dot(o, wo_s[...])
            o_ref[rows] = _layer_norm(ALPHA * h_ref[rows] + y, ln_g, ln_b)


def _cross_attn(h, batch, layer, ktmem, vmem, wq, wo, g, b):
    t = h.shape[0]
    per_batch = t // batch // TM
    seq_of = lambda i: jnp.maximum(i - WCHUNKS, 0) // per_batch
    kt_spec = pl.BlockSpec((D_MODEL, N_MEM), lambda i: (0, seq_of(i)))
    v_spec = pl.BlockSpec((N_MEM, D_MODEL), lambda i: (seq_of(i), 0))
    return pl.pallas_call(
        functools.partial(_ca_kernel, layer=layer),
        grid=(WCHUNKS + t // TM,),
        in_specs=[_token_rows(TM), kt_spec, v_spec, _staged(wq.shape, layer),
                  _staged(wo.shape, layer), _resident(g.shape), _resident(b.shape)],
        out_specs=_token_rows(TM),
        out_shape=jax.ShapeDtypeStruct((t, D_MODEL), F32),
        scratch_shapes=[pltpu.VMEM(wq.shape[1:], BF16), pltpu.VMEM(wo.shape[1:], BF16)],
        compiler_params=pltpu.CompilerParams(
            dimension_semantics=("arbitrary",), vmem_limit_bytes=VMEM_LIMIT_BYTES),
        name="cross_attn_ln",
    )(h, ktmem, vmem, wq, wo, g, b)


def _pool_block_diag(w):
    ng, c, d = w.shape
    out = jnp.zeros((ng * c, ng * d), w.dtype)
    for gi in range(ng):
        out = out.at[gi * c:(gi + 1) * c, gi * d:(gi + 1) * d].set(w[gi])
    return out


def kernel(x, mem, ffn1_gate, ffn1_up, ffn1_down, w_in, conv_w, rec_lb, rec_norm_g, pool_w,
           pool_scale, w_out, ca_q, ca_k, ca_v, ca_o, ffn2_gate, ffn2_up, ffn2_down, ln_g, ln_b):
    batch, seq, _ = x.shape
    h = x.reshape(batch * seq, D_MODEL)
    mem2 = mem.reshape(batch * N_MEM, D_MODEL)
    for l in range(DEPTH):
        h = _ffn(h, l, 0, ffn1_gate, ffn1_up, ffn1_down, ln_g, ln_b)
        h = _mixer(h, batch, l, w_in, conv_w, rec_lb, rec_norm_g,
                   _pool_block_diag(pool_w[l]).astype(BF16), pool_scale, w_out, ln_g, ln_b)
        ktmem, vmem = _kv(mem2, l, ca_k, ca_v)
        h = _cross_attn(h, batch, l, ktmem, vmem, ca_q, ca_o, ln_g, ln_b)
        h = _ffn(h, l, 3, ffn2_gate, ffn2_up, ffn2_down, ln_g, ln_b)
    return h.reshape(batch, seq, D_MODEL)
```

```python
import functools

import jax
import jax.numpy as jnp
from jax import lax
from jax.experimental import pallas as pl
from jax.experimental.pallas import tpu as pltpu

D_MODEL = 1024
DEPTH = 2
N_MEM = 256
CONV_DIM = 256
CONV_WIDTH = 3
REC_HEADS = 4
REC_DIM = 512
REC_HEAD_DIM = 128
REC_FDIM = 512
POOL_DIM = 256
POOL_WINDOWS = (2, 4, 8, 16)
POOL_GROUP = 64
D_IN = 3 * CONV_DIM + 2 * REC_FDIM + 2 * REC_DIM + POOL_DIM
CA_HEADS = 4
CA_HEAD_DIM = 256
D_FF = 2816
ALPHA = (2.0 * DEPTH) ** 0.25
LN_EPS = 1e-5
RMS_EPS = 1e-6

F32 = jnp.float32
BF16 = jnp.bfloat16

VMEM_LIMIT_BYTES = 56 * 1024 * 1024
SUBLANES = 8
TM = 1024
CA_TM = 512
SUB = 256
FFN_WCHUNKS = 4
MIX_WCHUNKS = 4
CA_WCHUNKS = 2
TS = 256
HALF = TS // 2
PAIR = 2 * REC_HEAD_DIM
CONV_HIST = 8
POOL_HIST = 16
LEVELS = (1, 2, 4, 8, 16, 32, 64, 128)
FILL_COLS = 256
FILL_PLAN_FIRST = (2, 1) + (1, 1, 1, 1, 1, 0, 0, 0) + (0, 0, 0, 0) + (1, 1, 1, 1)
FILL_PLAN_LATER = (0, 0) + (1, 1, 1, 1, 1, 1, 1, 1) + (0, 0, 0, 0) + (1, 1, 1, 1)
MIX_BLOCKS = 2

assert HALF == REC_HEAD_DIM and D_IN % FILL_COLS == 0


def _layer_norm(y, g, b):
    mu = jnp.mean(y, axis=-1, keepdims=True)
    d = y - mu
    var = jnp.mean(d * d, axis=-1, keepdims=True)
    return d * lax.rsqrt(var + LN_EPS) * g + b


def _dot(a, b):
    return jnp.dot(a, b, preferred_element_type=F32)


def _dot_nt(a, b):
    return lax.dot_general(a, b, (((1,), (1,)), ((), ())), preferred_element_type=F32)


def _resident(shape):
    return pl.BlockSpec(shape, lambda *_: (0,) * len(shape), pipeline_mode=pl.Buffered(1))


def _staged(stacked_shape, layer, nchunks):
    _, rows, cols = stacked_shape
    return pl.BlockSpec((None, rows // nchunks, cols),
                        lambda i: (layer, jnp.minimum(i, nchunks - 1), 0))


def _ln_params(g_ref, b_ref, layer, idx):
    return g_ref[layer, idx:idx + 1, :], b_ref[layer, idx:idx + 1, :]


def _token_rows(rows, nchunks):
    return pl.BlockSpec((rows, D_MODEL), lambda i: (jnp.maximum(i - nchunks, 0), 0))


def _sub_blocks(step_rows=TM):
    return [slice(lo, lo + SUB) for lo in range(0, step_rows, SUB)]


def _stage_chunk(i, src_ref, dst_ref, scale=None):
    rows = src_ref.shape[0]
    w = src_ref[...] if scale is None else src_ref[...] * scale
    dst_ref[pl.ds(pl.multiple_of(i * rows, rows), rows), :] = w.astype(BF16)


def _ffn_kernel(h_ref, wg_ref, wu_ref, wd_ref, g_ref, b_ref, o_ref, wg_s, wu_s, wd_s,
                *, layer, ln_idx):
    step = pl.program_id(0)
    ln_g, ln_b = _ln_params(g_ref, b_ref, layer, ln_idx)

    @pl.when(step < FFN_WCHUNKS)
    def _():
        _stage_chunk(step, wg_ref, wg_s)
        _stage_chunk(step, wu_ref, wu_s)
        _stage_chunk(step, wd_ref, wd_s, scale=0.5)

    def up_stage(rows):
        xb = h_ref[rows].astype(BF16)
        gate = _dot(xb, wg_s[...])
        up = _dot(xb, wu_s[...])
        return (gate * jax.nn.sigmoid(gate) * up).astype(BF16)

    def down_stage(rows, act):
        y = _dot(act, wd_s[...])
        o_ref[rows] = _layer_norm(ALPHA * h_ref[rows] + y, ln_g, ln_b)

    @pl.when(step >= FFN_WCHUNKS)
    def _():
        blocks = _sub_blocks()
        act = up_stage(blocks[0])
        for prev, rows in zip(blocks, blocks[1:]):
            nxt = up_stage(rows)
            down_stage(prev, act)
            act = nxt
        down_stage(blocks[-1], act)


def _ffn(h, layer, ln_idx, wg, wu, wd, g, b):
    t = h.shape[0]
    return pl.pallas_call(
        functools.partial(_ffn_kernel, layer=layer, ln_idx=ln_idx),
        grid=(FFN_WCHUNKS + t // TM,),
        in_specs=[_token_rows(TM, FFN_WCHUNKS)]
                 + [_staged(w.shape, layer, FFN_WCHUNKS) for w in (wg, wu, wd)]
                 + [_resident(g.shape), _resident(b.shape)],
        out_specs=_token_rows(TM, FFN_WCHUNKS),
        out_shape=jax.ShapeDtypeStruct((t, D_MODEL), F32),
        scratch_shapes=[pltpu.VMEM(wg.shape[1:], BF16), pltpu.VMEM(wu.shape[1:], BF16),
                        pltpu.VMEM(wd.shape[1:], BF16)],
        compiler_params=pltpu.CompilerParams(
            dimension_semantics=("arbitrary",), vmem_limit_bytes=VMEM_LIMIT_BYTES),
        name="ffn_ln",
    )(h, wg, wu, wd, g, b)


def _mixer_tile(x, z, tile_idx, layer, fill, convw_ref, reclb_ref, normg_ref, poolw_ref,
                poolscale_ref, wout_ref, g_ref, beta_ref, state_ref, uprev_ref, pprev_ref, bcum_ref):
    def proj(lo, width):
        return z[:, lo:lo + width]

    fill()

    c0 = 0
    cb = proj(c0, CONV_DIM)
    cc = proj(c0 + CONV_DIM, CONV_DIM)
    ch = proj(c0 + 2 * CONV_DIM, CONV_DIM)
    u = cc * ch
    uext = jnp.concatenate([uprev_ref[...], u], axis=0)
    u1 = pltpu.roll(uext, 1, axis=0)[CONV_HIST:]
    u2 = pltpu.roll(uext, 2, axis=0)[CONV_HIST:]
    uprev_ref[...] = u[TS - CONV_HIST:]
    cw = convw_ref[layer]
    y_conv = cb * (cw[2:3] * u + cw[1:2] * u1 + cw[0:1] * u2)

    p0 = 3 * CONV_DIM + 2 * REC_FDIM + 2 * REC_DIM
    pu = proj(p0, POOL_DIM)
    pext = jnp.concatenate([pprev_ref[...], pu], axis=0)
    pprev_ref[...] = pu[TS - POOL_HIST:]
    lane = lax.broadcasted_iota(jnp.int32, (TS, POOL_DIM), 1)
    tpos = tile_idx * TS + lax.broadcasted_iota(jnp.int32, (TS, POOL_DIM), 0)
    acc = pext
    wsum = None
    win = None
    span = 1
    for gi, w in enumerate(POOL_WINDOWS):
        while span < w:
            acc = acc + pltpu.roll(acc, span, axis=0)
            span *= 2
        part = acc[POOL_HIST:]
        in_group = lane >= gi * POOL_GROUP
        wsum = part if wsum is None else jnp.where(in_group, part, wsum)
        win = jnp.full((TS, POOL_DIM), w, jnp.int32) if win is None else jnp.where(in_group, w, win)
    count = jnp.minimum(tpos + 1, win).astype(F32)
    pooled = (wsum / count - pu).astype(BF16)
    y_pool = _dot(pooled, poolw_ref[...]) * poolscale_ref[layer:layer + 1, :]
    fill()

    r0 = 3 * CONV_DIM
    rq = proj(r0, REC_FDIM)
    rf = proj(r0 + REC_FDIM, REC_FDIM)
    rv = proj(r0 + 2 * REC_FDIM, REC_DIM)
    rg = proj(r0 + 2 * REC_FDIM + REC_DIM, REC_DIM)

    rl = reclb_ref[...]
    e = jnp.exp(rl - jnp.max(rl, axis=0, keepdims=True))
    sm = e / jnp.sum(e, axis=0, keepdims=True)
    lb = jnp.sum(sm[0:layer + 1], axis=0, keepdims=True) - sm[0:1]

    f = lb + (1.0 - lb) * jax.nn.sigmoid(rf)
    log2f = jnp.log2(f)
    k = 1.0 - f

    row2 = lax.broadcasted_iota(jnp.int32, (TS, TS), 0)
    col2 = lax.broadcasted_iota(jnp.int32, (TS, TS), 1)
    tri = jnp.where(row2 >= col2, 1.0, 0.0).astype(BF16)
    l_hi = log2f.astype(BF16)
    l_lo = (log2f - l_hi.astype(F32)).astype(BF16)
    bcum = _dot(tri, l_hi) + _dot(tri, l_lo)
    bcum_ref[...] = bcum
    b_last = bcum_ref[TS - 1:TS, :]

    q_in = (rq * jnp.exp2(bcum)).astype(BF16)
    k_out = (k * jnp.exp2(b_last - bcum)).astype(BF16)
    decay_all = jnp.exp2(b_last)

    rbit = lax.broadcasted_iota(jnp.int32, (TS, REC_FDIM), 0)
    b3 = bcum.reshape(TS // SUBLANES, SUBLANES, REC_FDIM)
    sub3 = lax.broadcasted_iota(jnp.int32, b3.shape, 1)

    def level_operand(half):
        if half >= SUBLANES:
            parts = []
            for gidx in range(TS // (2 * half)):
                lo = 2 * half * gidx
                ref = bcum_ref[lo + half - 1:lo + half, :]
                parts.append(k[lo:lo + half] * jnp.exp2(ref - bcum[lo:lo + half]))
                parts.append(rq[lo + half:lo + 2 * half]
                             * jnp.exp2(bcum[lo + half:lo + 2 * half] - ref))
            return jnp.concatenate(parts, axis=0)
        if half == 1:
            ref = jnp.where((sub3 & 1) == 1, pltpu.roll(b3, 1, axis=1), b3)
        elif half == 2:
            m = sub3 & 3
            ref = jnp.where(m == 0, pltpu.roll(b3, SUBLANES - 1, axis=1),
                            jnp.where(m == 1, b3,
                                      jnp.where(m == 2, pltpu.roll(b3, 1, axis=1),
                                                pltpu.roll(b3, 2, axis=1))))
        else:
            ref = jnp.concatenate(
                [jnp.broadcast_to(bcum_ref[SUBLANES * gidx + half - 1:SUBLANES * gidx + half, :],
                                  (1, SUBLANES, REC_FDIM))
                 for gidx in range(TS // SUBLANES)], axis=0)
        gfac = jnp.exp2(-jnp.abs(b3 - ref)).reshape(TS, REC_FDIM)
        return jnp.where((rbit & half) != 0, rq, k) * gfac

    w_lvl = {}
    for half in LEVELS:
        fill()
        w_lvl[half] = level_operand(half).astype(BF16)

    trow = lax.broadcasted_iota(jnp.int32, (HALF, PAIR), 0)
    tcol = lax.broadcasted_iota(jnp.int32, (HALF, PAIR), 1) & (HALF - 1)
    xor = trow ^ tcol
    causal = trow > tcol
    lvl_mask = {half: xor >= half for half in LEVELS[:-1]}
    zero_pair = jnp.zeros((HALF, PAIR), F32)
    zero_blk = jnp.zeros((HALF, HALF), F32)
    zero_key = jnp.zeros((HALF, REC_HEAD_DIM), BF16)

    def pair_nt(lhs, keys):
        rhs = jnp.concatenate(
            [jnp.concatenate([keys[:, :REC_HEAD_DIM], zero_key], axis=1),
             jnp.concatenate([zero_key, keys[:, REC_HEAD_DIM:]], axis=1)], axis=0)
        return _dot_nt(lhs, rhs)

    attn = []
    for pr in range(REC_HEADS // 2):
        lanes = slice(pr * PAIR, (pr + 1) * PAIR)
        diag = []
        for d in range(2):
            rows = slice(d * HALF, (d + 1) * HALF)
            blk = zero_pair
            for half in LEVELS[:-1]:
                wl = w_lvl[half][rows, lanes]
                blk = jnp.where(lvl_mask[half], pair_nt(wl, wl), blk)
            diag.append(jnp.where(causal, blk, 0.0))
            fill()
        wl = w_lvl[HALF]
        cross = pair_nt(wl[HALF:, lanes], wl[:HALF, lanes])
        for j in range(2):
            cs = slice(j * HALF, (j + 1) * HALF)
            attn.append(jnp.concatenate(
                [jnp.concatenate([diag[0][:, cs], zero_blk], axis=1),
                 jnp.concatenate([cross[:, cs], diag[1][:, cs]], axis=1)], axis=0).astype(BF16))

    sg = jax.nn.sigmoid(rg)
    ng = normg_ref[layer:layer + 1, :]
    y_rec = []
    for hd in range(REC_HEADS):
        fill()
        sl = slice(hd * REC_HEAD_DIM, (hd + 1) * REC_HEAD_DIM)
        qh = rq[:, sl]
        kh = k[:, sl]
        vh = rv[:, sl]
        st = state_ref[hd]
        o = (_dot(attn[hd], vh.astype(BF16))
             + jnp.sum(qh * kh, axis=-1, keepdims=True) * vh
             + _dot_nt(q_in[:, sl], st.astype(BF16)))
        state_ref[hd] = st * decay_all[:, sl] + _dot(vh.T.astype(BF16), k_out[:, sl])
        o = o * lax.rsqrt(jnp.mean(o * o, axis=-1, keepdims=True) + RMS_EPS)
        y_rec.append(o * ng[:, sl] * sg[:, sl])

    mix = jnp.concatenate([y_conv] + y_rec + [y_pool], axis=-1).astype(BF16)
    y = _dot(mix, wout_ref[...])
    return _layer_norm(ALPHA * x + y, *_ln_params(g_ref, beta_ref, layer, 1))


def _mixer_kernel(h_ref, hnext_ref, win_ref, convw_ref, reclb_ref, normg_ref, poolw_ref,
                  poolscale_ref, wout_ref, g_ref, beta_ref, o_ref,
                  win_s, wout_s, z0_ref, z1_ref, state_ref, uprev_ref, pprev_ref, bcum_ref,
                  *, layer, steps_per_seq):
    step = pl.program_id(0)

    @pl.when(step < MIX_WCHUNKS)
    def _():
        _stage_chunk(step, win_ref, win_s)
        _stage_chunk(step, wout_ref, wout_s)

    @pl.when(step >= MIX_WCHUNKS)
    def _():
        s_idx = lax.rem(step - MIX_WCHUNKS, steps_per_seq)

        @pl.when(s_idx == 0)
        def _():
            state_ref[...] = jnp.zeros_like(state_ref)
            uprev_ref[...] = jnp.zeros_like(uprev_ref)
            pprev_ref[...] = jnp.zeros_like(pprev_ref)

        @pl.when(step == MIX_WCHUNKS)
        def _():
            z0_ref[...] = _dot(h_ref[0:TS].astype(BF16), win_s[...])

        tile_refs = (convw_ref, reclb_ref, normg_ref, poolw_ref, poolscale_ref, wout_s, g_ref,
                     beta_ref, state_ref, uprev_ref, pprev_ref, bcum_ref)

        def projector(xb, dst_ref, plan):
            todo = list(range(0, D_IN, FILL_COLS))
            counts = list(plan)

            def fill(flush=False):
                for _ in range(len(todo) if flush else counts.pop(0)):
                    lo = todo.pop(0)
                    dst_ref[:, lo:lo + FILL_COLS] = _dot(xb, win_s[:, lo:lo + FILL_COLS])

            return fill

        z_refs = (z0_ref, z1_ref)
        for j in range(MIX_BLOCKS):
            rows = slice(j * TS, (j + 1) * TS)
            if j + 1 < MIX_BLOCKS:
                upcoming = h_ref[(j + 1) * TS:(j + 2) * TS]
            else:
                upcoming = hnext_ref[...]
            fill = projector(upcoming.astype(BF16), z_refs[(j + 1) % 2],
                             FILL_PLAN_FIRST if j == 0 else FILL_PLAN_LATER)
            o_ref[rows] = _mixer_tile(h_ref[rows], z_refs[j % 2][...], MIX_BLOCKS * s_idx + j,
                                      layer, fill, *tile_refs)
            fill(flush=True)


def _mixer(h, batch, layer, w_in, conv_w, rec_lb, norm_g, pool_w, pool_scale, w_out, g, b):
    t = h.shape[0]
    steps_per_seq = t // batch // (MIX_BLOCKS * TS)
    last_tile = t // TS - 1
    nxt = pl.BlockSpec(
        (TS, D_MODEL),
        lambda i: (jnp.minimum(MIX_BLOCKS * (jnp.maximum(i - MIX_WCHUNKS, 0) + 1), last_tile), 0))
    small = (conv_w, rec_lb, norm_g, pool_w, pool_scale)
    return pl.pallas_call(
        functools.partial(_mixer_kernel, layer=layer, steps_per_seq=steps_per_seq),
        grid=(MIX_WCHUNKS + batch * steps_per_seq,),
        in_specs=([_token_rows(MIX_BLOCKS * TS, MIX_WCHUNKS), nxt,
                   _staged(w_in.shape, layer, MIX_WCHUNKS)]
                  + [_resident(p.shape) for p in small]
                  + [_staged(w_out.shape, layer, MIX_WCHUNKS), _resident(g.shape),
                     _resident(b.shape)]),
        out_specs=_token_rows(MIX_BLOCKS * TS, MIX_WCHUNKS),
        out_shape=jax.ShapeDtypeStruct((t, D_MODEL), F32),
        scratch_shapes=[
            pltpu.VMEM(w_in.shape[1:], BF16),
            pltpu.VMEM(w_out.shape[1:], BF16),
            pltpu.VMEM((TS, D_IN), F32),
            pltpu.VMEM((TS, D_IN), F32),
            pltpu.VMEM((REC_HEADS, REC_HEAD_DIM, REC_HEAD_DIM), F32),
            pltpu.VMEM((CONV_HIST, CONV_DIM), F32),
            pltpu.VMEM((POOL_HIST, POOL_DIM), F32),
            pltpu.VMEM((TS, REC_FDIM), F32),
        ],
        compiler_params=pltpu.CompilerParams(
            dimension_semantics=("arbitrary",), vmem_limit_bytes=VMEM_LIMIT_BYTES),
        name="mixer_ln",
    )(h, h, w_in, *small, w_out, g, b)


def _ca_kernel(h_ref, mem_ref, wk_ref, wv_ref, wq_ref, wo_ref, g_ref, b_ref, o_ref,
               wq_s, wo_s, kacc_ref, vacc_ref, kt_s, v_s, *, layer, steps_per_seq):
    step = pl.program_id(0)
    ln_g, ln_b = _ln_params(g_ref, b_ref, layer, 2)

    @pl.when(step < CA_WCHUNKS)
    def _():
        _stage_chunk(step, wq_ref, wq_s, scale=CA_HEAD_DIM ** -0.5)
        _stage_chunk(step, wo_ref, wo_s)

        mb = mem_ref[...].astype(BF16)
        k_part = _dot(mb, wk_ref[...].astype(BF16))
        v_part = _dot(mb, wv_ref[...].astype(BF16))

        @pl.when(step == 0)
        def _():
            kacc_ref[...] = k_part
            vacc_ref[...] = v_part

        @pl.when(step > 0)
        def _():
            kacc_ref[...] += k_part
            vacc_ref[...] += v_part

        @pl.when(step == CA_WCHUNKS - 1)
        def _():
            for s in range(kt_s.shape[0]):
                rows = slice(s * N_MEM, (s + 1) * N_MEM)
                kt_s[s] = kacc_ref[rows].T.astype(BF16)
                v_s[s] = vacc_ref[rows].astype(BF16)

    head_slices = [slice(hd * CA_HEAD_DIM, (hd + 1) * CA_HEAD_DIM) for hd in range(CA_HEADS)]

    def q_proj(rows):
        return _dot(h_ref[rows].astype(BF16), wq_s[...]).astype(BF16)

    @pl.when(step >= CA_WCHUNKS)
    def _():
        seq = (step - CA_WCHUNKS) // steps_per_seq

        def score(q):
            return [_dot(q[:, sl], kt_s[seq, sl, :]) for sl in head_slices]

        blocks = _sub_blocks(CA_TM)
        scores = score(q_proj(blocks[0]))
        q_next = q_proj(blocks[1]) if len(blocks) > 1 else None
        for i, rows in enumerate(blocks):
            heads = []
            for s, sl in zip(scores, head_slices):
                p = jnp.exp(s - jnp.max(s, axis=-1, keepdims=True))
                p = p * (1.0 / jnp.sum(p, axis=-1, keepdims=True))
                heads.append(_dot(p.astype(BF16), v_s[seq, :, sl]))
            if i + 1 < len(blocks):
                scores = score(q_next)
            if i + 2 < len(blocks):
                q_next = q_proj(blocks[i + 2])
            o = jnp.concatenate(heads, axis=-1).astype(BF16)
            y = _dot(o, wo_s[...])
            o_ref[rows] = _layer_norm(ALPHA * h_ref[rows] + y, ln_g, ln_b)


def _cross_attn(h, batch, layer, mem, wk, wv, wq, wo, g, b):
    t = h.shape[0]
    m = mem.shape[0]
    mem_spec = pl.BlockSpec((m, D_MODEL // CA_WCHUNKS),
                            lambda i: (0, jnp.minimum(i, CA_WCHUNKS - 1)))
    staged = lambda w: _staged(w.shape, layer, CA_WCHUNKS)
    return pl.pallas_call(
        functools.partial(_ca_kernel, layer=layer, steps_per_seq=t // batch // CA_TM),
        grid=(CA_WCHUNKS + t // CA_TM,),
        in_specs=[_token_rows(CA_TM, CA_WCHUNKS), mem_spec, staged(wk), staged(wv), staged(wq),
                  staged(wo), _resident(g.shape), _resident(b.shape)],
        out_specs=_token_rows(CA_TM, CA_WCHUNKS),
        out_shape=jax.ShapeDtypeStruct((t, D_MODEL), F32),
        scratch_shapes=[
            pltpu.VMEM(wq.shape[1:], BF16),
            pltpu.VMEM(wo.shape[1:], BF16),
            pltpu.VMEM((m, D_MODEL), F32),
            pltpu.VMEM((m, D_MODEL), F32),
            pltpu.VMEM((batch, D_MODEL, N_MEM), BF16),
            pltpu.VMEM((batch, N_MEM, D_MODEL), BF16),
        ],
        compiler_params=pltpu.CompilerParams(
            dimension_semantics=("arbitrary",), vmem_limit_bytes=VMEM_LIMIT_BYTES),
        name="cross_attn_ln",
    )(h, mem, wk, wv, wq, wo, g, b)


def _pool_block_diag(w):
    ng, c, d = w.shape
    out = jnp.zeros((ng * c, ng * d), w.dtype)
    for gi in range(ng):
        out = out.at[gi * c:(gi + 1) * c, gi * d:(gi + 1) * d].set(w[gi])
    return out


def kernel(x, mem, ffn1_gate, ffn1_up, ffn1_down, w_in, conv_w, rec_lb, rec_norm_g, pool_w,
           pool_scale, w_out, ca_q, ca_k, ca_v, ca_o, ffn2_gate, ffn2_up, ffn2_down, ln_g, ln_b):
    batch, seq, _ = x.shape
    h = x.reshape(batch * seq, D_MODEL)
    mem2 = mem.reshape(batch * N_MEM, D_MODEL)
    for l in range(DEPTH):
        h = _ffn(h, l, 0, ffn1_gate, ffn1_up, ffn1_down, ln_g, ln_b)
        h = _mixer(h, batch, l, w_in, conv_w, rec_lb, rec_norm_g,
                   _pool_block_diag(pool_w[l]).astype(BF16), pool_scale, w_out, ln_g, ln_b)
        h = _cross_attn(h, batch, l, mem2, ca_k, ca_v, ca_q, ca_o, ln_g, ln_b)
        h = _ffn(h, l, 3, ffn2_gate, ffn2_up, ffn2_down, ln_g, ln_b)
    return h.reshape(batch, seq, D_MODEL)
```

```python
import functools

import jax
import jax.numpy as jnp
from jax import lax
from jax.experimental import pallas as pl
from jax.experimental.pallas import tpu as pltpu

D_MODEL = 1024
DEPTH = 2
N_MEM = 256
CONV_DIM = 256
CONV_WIDTH = 3
REC_HEADS = 4
REC_DIM = 512
REC_HEAD_DIM = 128
REC_FDIM = 512
POOL_DIM = 256
POOL_WINDOWS = (2, 4, 8, 16)
POOL_GROUP = 64
D_IN = 3 * CONV_DIM + 2 * REC_FDIM + 2 * REC_DIM + POOL_DIM
CA_HEADS = 4
CA_HEAD_DIM = 256
D_FF = 2816
ALPHA = (2.0 * DEPTH) ** 0.25
LN_EPS = 1e-5
RMS_EPS = 1e-6

F32 = jnp.float32
BF16 = jnp.bfloat16

VMEM_LIMIT_BYTES = 56 * 1024 * 1024
SUBLANES = 8
TM = 1024
CA_TM = 512
SUB = 256
FFN_WCHUNKS = 4
MIX_WCHUNKS = 4
CA_WCHUNKS = 2
TS = 256
HALF = TS // 2
PAIR = 2 * REC_HEAD_DIM
CONV_HIST = 8
POOL_HIST = 16
LEVELS = (1, 2, 4, 8, 16, 32, 64, 128)
FILL_COLS = 256
_PAIR_PLAN_FIRST = (1, 1, 1, 0, 0, 0, 0, 0) + (0, 0) + (1, 1)
_PAIR_PLAN_LATER = (1, 1, 1, 1, 0, 0, 0, 0) + (0, 0) + (1, 1)
FILL_PLAN_FIRST = (2, 1) + _PAIR_PLAN_FIRST + (1, 1, 0, 0, 0, 0, 0, 0) + (0, 0) + (1, 1)
FILL_PLAN_LATER = (0, 0) + _PAIR_PLAN_LATER + _PAIR_PLAN_LATER
MIX_BLOCKS = 2

assert HALF == REC_HEAD_DIM and D_IN % FILL_COLS == 0


def _layer_norm(y, g, b):
    mu = jnp.mean(y, axis=-1, keepdims=True)
    d = y - mu
    var = jnp.mean(d * d, axis=-1, keepdims=True)
    return d * lax.rsqrt(var + LN_EPS) * g + b


def _dot(a, b):
    return jnp.dot(a, b, preferred_element_type=F32)


def _dot_nt(a, b):
    return lax.dot_general(a, b, (((1,), (1,)), ((), ())), preferred_element_type=F32)


def _resident(shape):
    return pl.BlockSpec(shape, lambda *_: (0,) * len(shape), pipeline_mode=pl.Buffered(1))


def _staged(stacked_shape, layer, nchunks):
    _, rows, cols = stacked_shape
    return pl.BlockSpec((None, rows // nchunks, cols),
                        lambda i: (layer, jnp.minimum(i, nchunks - 1), 0))


def _ln_params(g_ref, b_ref, layer, idx):
    return g_ref[layer, idx:idx + 1, :], b_ref[layer, idx:idx + 1, :]


def _token_rows(rows, nchunks):
    return pl.BlockSpec((rows, D_MODEL), lambda i: (jnp.maximum(i - nchunks, 0), 0))


def _sub_blocks(step_rows=TM):
    return [slice(lo, lo + SUB) for lo in range(0, step_rows, SUB)]


def _stage_chunk(i, src_ref, dst_ref, scale=None):
    rows = src_ref.shape[0]
    w = src_ref[...] if scale is None else src_ref[...] * scale
    dst_ref[pl.ds(pl.multiple_of(i * rows, rows), rows), :] = w.astype(BF16)


def _ffn_kernel(h_ref, wg_ref, wu_ref, wd_ref, g_ref, b_ref, o_ref, wg_s, wu_s, wd_s,
                *, layer, ln_idx):
    step = pl.program_id(0)
    ln_g, ln_b = _ln_params(g_ref, b_ref, layer, ln_idx)

    @pl.when(step < FFN_WCHUNKS)
    def _():
        _stage_chunk(step, wg_ref, wg_s)
        _stage_chunk(step, wu_ref, wu_s)
        _stage_chunk(step, wd_ref, wd_s, scale=0.5)

    def up_stage(rows):
        xb = h_ref[rows].astype(BF16)
        gate = _dot(xb, wg_s[...])
        up = _dot(xb, wu_s[...])
        return (gate * jax.nn.sigmoid(gate) * up).astype(BF16)

    def down_stage(rows, act):
        y = _dot(act, wd_s[...])
        o_ref[rows] = _layer_norm(ALPHA * h_ref[rows] + y, ln_g, ln_b)

    @pl.when(step >= FFN_WCHUNKS)
    def _():
        blocks = _sub_blocks()
        act = up_stage(blocks[0])
        for prev, rows in zip(blocks, blocks[1:]):
            nxt = up_stage(rows)
            down_stage(prev, act)
            act = nxt
        down_stage(blocks[-1], act)


def _ffn(h, layer, ln_idx, wg, wu, wd, g, b):
    t = h.shape[0]
    return pl.pallas_call(
        functools.partial(_ffn_kernel, layer=layer, ln_idx=ln_idx),
        grid=(FFN_WCHUNKS + t // TM,),
        in_specs=[_token_rows(TM, FFN_WCHUNKS)]
                 + [_staged(w.shape, layer, FFN_WCHUNKS) for w in (wg, wu, wd)]
                 + [_resident(g.shape), _resident(b.shape)],
        out_specs=_token_rows(TM, FFN_WCHUNKS),
        out_shape=jax.ShapeDtypeStruct((t, D_MODEL), F32),
        scratch_shapes=[pltpu.VMEM(wg.shape[1:], BF16), pltpu.VMEM(wu.shape[1:], BF16),
                        pltpu.VMEM(wd.shape[1:], BF16)],
        compiler_params=pltpu.CompilerParams(
            dimension_semantics=("arbitrary",), vmem_limit_bytes=VMEM_LIMIT_BYTES),
        name="ffn_ln",
    )(h, wg, wu, wd, g, b)


def _mixer_tile(x, z, tile_idx, layer, fill, convw_ref, reclb_ref, normg_ref, poolw_ref,
                poolscale_ref, wout_ref, g_ref, beta_ref, state_ref, uprev_ref, pprev_ref, bcum_ref):
    def proj(lo, width):
        return z[:, lo:lo + width]

    fill()

    c0 = 0
    cb = proj(c0, CONV_DIM)
    cc = proj(c0 + CONV_DIM, CONV_DIM)
    ch = proj(c0 + 2 * CONV_DIM, CONV_DIM)
    u = cc * ch
    uext = jnp.concatenate([uprev_ref[...], u], axis=0)
    u1 = pltpu.roll(uext, 1, axis=0)[CONV_HIST:]
    u2 = pltpu.roll(uext, 2, axis=0)[CONV_HIST:]
    uprev_ref[...] = u[TS - CONV_HIST:]
    cw = convw_ref[layer]
    y_conv = cb * (cw[2:3] * u + cw[1:2] * u1 + cw[0:1] * u2)

    p0 = 3 * CONV_DIM + 2 * REC_FDIM + 2 * REC_DIM
    pu = proj(p0, POOL_DIM)
    pext = jnp.concatenate([pprev_ref[...], pu], axis=0)
    pprev_ref[...] = pu[TS - POOL_HIST:]
    lane = lax.broadcasted_iota(jnp.int32, (TS, POOL_DIM), 1)
    tpos = tile_idx * TS + lax.broadcasted_iota(jnp.int32, (TS, POOL_DIM), 0)
    acc = pext
    wsum = None
    win = None
    span = 1
    for gi, w in enumerate(POOL_WINDOWS):
        while span < w:
            acc = acc + pltpu.roll(acc, span, axis=0)
            span *= 2
        part = acc[POOL_HIST:]
        in_group = lane >= gi * POOL_GROUP
        wsum = part if wsum is None else jnp.where(in_group, part, wsum)
        win = jnp.full((TS, POOL_DIM), w, jnp.int32) if win is None else jnp.where(in_group, w, win)
    count = jnp.minimum(tpos + 1, win).astype(F32)
    pooled = (wsum / count - pu).astype(BF16)
    y_pool = _dot(pooled, poolw_ref[layer]) * poolscale_ref[layer:layer + 1, :]
    fill()

    r0 = 3 * CONV_DIM
    rl = reclb_ref[...]
    e = jnp.exp(rl - jnp.max(rl, axis=0, keepdims=True))
    sm = e / jnp.sum(e, axis=0, keepdims=True)
    lb_all = jnp.sum(sm[0:layer + 1], axis=0, keepdims=True) - sm[0:1]

    row2 = lax.broadcasted_iota(jnp.int32, (TS, TS), 0)
    col2 = lax.broadcasted_iota(jnp.int32, (TS, TS), 1)
    tri = jnp.where(row2 >= col2, 1.0, 0.0).astype(BF16)
    rbit = lax.broadcasted_iota(jnp.int32, (TS, PAIR), 0)
    sub3 = lax.broadcasted_iota(jnp.int32, (TS // SUBLANES, SUBLANES, PAIR), 1)
    trow = lax.broadcasted_iota(jnp.int32, (HALF, PAIR), 0)
    tcol = lax.broadcasted_iota(jnp.int32, (HALF, PAIR), 1) & (HALF - 1)
    xor = trow ^ tcol
    causal = trow > tcol
    lvl_mask = {half: xor >= half for half in LEVELS[:-1]}
    zero_pair = jnp.zeros((HALF, PAIR), F32)
    zero_blk = jnp.zeros((HALF, HALF), F32)
    zero_key = jnp.zeros((HALF, REC_HEAD_DIM), BF16)

    def pair_nt(lhs, keys):
        rhs = jnp.concatenate(
            [jnp.concatenate([keys[:, :REC_HEAD_DIM], zero_key], axis=1),
             jnp.concatenate([zero_key, keys[:, REC_HEAD_DIM:]], axis=1)], axis=0)
        return _dot_nt(lhs, rhs)

    y_rec = []
    for pr in range(REC_HEADS // 2):
        lanes = slice(pr * PAIR, (pr + 1) * PAIR)
        rq = proj(r0 + pr * PAIR, PAIR)
        rf = proj(r0 + REC_FDIM + pr * PAIR, PAIR)
        rv = proj(r0 + 2 * REC_FDIM + pr * PAIR, PAIR)
        rg = proj(r0 + 2 * REC_FDIM + REC_DIM + pr * PAIR, PAIR)
        lb = lb_all[:, lanes]

        f = lb + (1.0 - lb) * jax.nn.sigmoid(rf)
        log2f = jnp.log2(f)
        k = 1.0 - f

        l_hi = log2f.astype(BF16)
        l_lo = (log2f - l_hi.astype(F32)).astype(BF16)
        bcum = _dot(tri, l_hi) + _dot(tri, l_lo)
        bcum_ref[:, lanes] = bcum
        b_last = bcum_ref[TS - 1:TS, lanes]

        q_in = (rq * jnp.exp2(bcum)).astype(BF16)
        k_out = (k * jnp.exp2(b_last - bcum)).astype(BF16)
        decay_all = jnp.exp2(b_last)

        b3 = bcum.reshape(TS // SUBLANES, SUBLANES, PAIR)

        def level_operand(half):
            if half >= SUBLANES:
                parts = []
                for gidx in range(TS // (2 * half)):
                    lo = 2 * half * gidx
                    ref = bcum_ref[lo + half - 1:lo + half, lanes]
                    parts.append(k[lo:lo + half] * jnp.exp2(ref - bcum[lo:lo + half]))
                    parts.append(rq[lo + half:lo + 2 * half]
                                 * jnp.exp2(bcum[lo + half:lo + 2 * half] - ref))
                return jnp.concatenate(parts, axis=0)
            if half == 1:
                ref = jnp.where((sub3 & 1) == 1, pltpu.roll(b3, 1, axis=1), b3)
            elif half == 2:
                m = sub3 & 3
                ref = jnp.where(m == 0, pltpu.roll(b3, SUBLANES - 1, axis=1),
                                jnp.where(m == 1, b3,
                                          jnp.where(m == 2, pltpu.roll(b3, 1, axis=1),
                                                    pltpu.roll(b3, 2, axis=1))))
            else:
                ref = jnp.concatenate(
                    [jnp.broadcast_to(
                        bcum_ref[SUBLANES * gidx + half - 1:SUBLANES * gidx + half, lanes],
                        (1, SUBLANES, PAIR))
                     for gidx in range(TS // SUBLANES)], axis=0)
            gfac = jnp.exp2(-jnp.abs(b3 - ref)).reshape(TS, PAIR)
            return jnp.where((rbit & half) != 0, rq, k) * gfac

        w_lvl = {}
        for half in LEVELS:
            fill()
            w_lvl[half] = level_operand(half).astype(BF16)

        diag = []
        for d in range(2):
            rows = slice(d * HALF, (d + 1) * HALF)
            blk = zero_pair
            for half in LEVELS[:-1]:
                wl = w_lvl[half][rows]
                blk = jnp.where(lvl_mask[half], pair_nt(wl, wl), blk)
            diag.append(jnp.where(causal, blk, 0.0))
            fill()
        wl = w_lvl[HALF]
        cross = pair_nt(wl[HALF:], wl[:HALF])

        sg = jax.nn.sigmoid(rg)
        ng = normg_ref[layer:layer + 1, lanes]
        for j in range(2):
            fill()
            sl = slice(j * REC_HEAD_DIM, (j + 1) * REC_HEAD_DIM)
            attn = jnp.concatenate(
                [jnp.concatenate([diag[0][:, sl], zero_blk], axis=1),
                 jnp.concatenate([cross[:, sl], diag[1][:, sl]], axis=1)], axis=0).astype(BF16)
            qh = rq[:, sl]
            kh = k[:, sl]
            vh = rv[:, sl]
            st = state_ref[2 * pr + j]
            o = (_dot(attn, vh.astype(BF16))
                 + jnp.sum(qh * kh, axis=-1, keepdims=True) * vh
                 + _dot_nt(q_in[:, sl], st.astype(BF16)))
            state_ref[2 * pr + j] = (st * decay_all[:, sl]
                                     + _dot(vh.T.astype(BF16), k_out[:, sl]))
            o = o * lax.rsqrt(jnp.mean(o * o, axis=-1, keepdims=True) + RMS_EPS)
            y_rec.append(o * ng[:, sl] * sg[:, sl])

    mix = jnp.concatenate([y_conv] + y_rec + [y_pool], axis=-1).astype(BF16)
    y = _dot(mix, wout_ref[...])
    return _layer_norm(ALPHA * x + y, *_ln_params(g_ref, beta_ref, layer, 1))


def _mixer_kernel(h_ref, hnext_ref, win_ref, convw_ref, reclb_ref, normg_ref, poolw_ref,
                  poolscale_ref, wout_ref, g_ref, beta_ref, o_ref,
                  win_s, wout_s, z0_ref, z1_ref, state_ref, uprev_ref, pprev_ref, bcum_ref,
                  *, layer, steps_per_seq):
    step = pl.program_id(0)

    @pl.when(step < MIX_WCHUNKS)
    def _():
        _stage_chunk(step, win_ref, win_s)
        _stage_chunk(step, wout_ref, wout_s)

    @pl.when(step >= MIX_WCHUNKS)
    def _():
        s_idx = lax.rem(step - MIX_WCHUNKS, steps_per_seq)

        @pl.when(s_idx == 0)
        def _():
            state_ref[...] = jnp.zeros_like(state_ref)
            uprev_ref[...] = jnp.zeros_like(uprev_ref)
            pprev_ref[...] = jnp.zeros_like(pprev_ref)

        @pl.when(step == MIX_WCHUNKS)
        def _():
            z0_ref[...] = _dot(h_ref[0:TS].astype(BF16), win_s[...])

        tile_refs = (convw_ref, reclb_ref, normg_ref, poolw_ref, poolscale_ref, wout_s, g_ref,
                     beta_ref, state_ref, uprev_ref, pprev_ref, bcum_ref)

        def projector(xb, dst_ref, plan):
            todo = list(range(0, D_IN, FILL_COLS))
            counts = list(plan)

            def fill(flush=False):
                for _ in range(len(todo) if flush else counts.pop(0)):
                    lo = todo.pop(0)
                    dst_ref[:, lo:lo + FILL_COLS] = _dot(xb, win_s[:, lo:lo + FILL_COLS])

            return fill

        z_refs = (z0_ref, z1_ref)
        for j in range(MIX_BLOCKS):
            rows = slice(j * TS, (j + 1) * TS)
            if j + 1 < MIX_BLOCKS:
                upcoming = h_ref[(j + 1) * TS:(j + 2) * TS]
            else:
                upcoming = hnext_ref[...]
            fill = projector(upcoming.astype(BF16), z_refs[(j + 1) % 2],
                             FILL_PLAN_FIRST if j == 0 else FILL_PLAN_LATER)
            o_ref[rows] = _mixer_tile(h_ref[rows], z_refs[j % 2][...], MIX_BLOCKS * s_idx + j,
                                      layer, fill, *tile_refs)
            fill(flush=True)


def _mixer(h, batch, layer, w_in, conv_w, rec_lb, norm_g, pool_w, pool_scale, w_out, g, b):
    t = h.shape[0]
    steps_per_seq = t // batch // (MIX_BLOCKS * TS)
    last_tile = t // TS - 1
    nxt = pl.BlockSpec(
        (TS, D_MODEL),
        lambda i: (jnp.minimum(MIX_BLOCKS * (jnp.maximum(i - MIX_WCHUNKS, 0) + 1), last_tile), 0))
    small = (conv_w, rec_lb, norm_g, pool_w, pool_scale)
    return pl.pallas_call(
        functools.partial(_mixer_kernel, layer=layer, steps_per_seq=steps_per_seq),
        grid=(MIX_WCHUNKS + batch * steps_per_seq,),
        in_specs=([_token_rows(MIX_BLOCKS * TS, MIX_WCHUNKS), nxt,
                   _staged(w_in.shape, layer, MIX_WCHUNKS)]
                  + [_resident(p.shape) for p in small]
                  + [_staged(w_out.shape, layer, MIX_WCHUNKS), _resident(g.shape),
                     _resident(b.shape)]),
        out_specs=_token_rows(MIX_BLOCKS * TS, MIX_WCHUNKS),
        out_shape=jax.ShapeDtypeStruct((t, D_MODEL), F32),
        scratch_shapes=[
            pltpu.VMEM(w_in.shape[1:], BF16),
            pltpu.VMEM(w_out.shape[1:], BF16),
            pltpu.VMEM((TS, D_IN), F32),
            pltpu.VMEM((TS, D_IN), F32),
            pltpu.VMEM((REC_HEADS, REC_HEAD_DIM, REC_HEAD_DIM), F32),
            pltpu.VMEM((CONV_HIST, CONV_DIM), F32),
            pltpu.VMEM((POOL_HIST, POOL_DIM), F32),
            pltpu.VMEM((TS, REC_FDIM), F32),
        ],
        compiler_params=pltpu.CompilerParams(
            dimension_semantics=("arbitrary",), vmem_limit_bytes=VMEM_LIMIT_BYTES),
        name="mixer_ln",
    )(h, h, w_in, *small, w_out, g, b)


def _ca_kernel(h_ref, mem_ref, wk_ref, wv_ref, wq_ref, wo_ref, g_ref, b_ref, o_ref,
               wq_s, wo_s, kacc_ref, vacc_ref, kt_s, v_s, *, layer, steps_per_seq):
    step = pl.program_id(0)
    ln_g, ln_b = _ln_params(g_ref, b_ref, layer, 2)

    @pl.when(step < CA_WCHUNKS)
    def _():
        _stage_chunk(step, wq_ref, wq_s, scale=CA_HEAD_DIM ** -0.5)
        _stage_chunk(step, wo_ref, wo_s)

        mb = mem_ref[...].astype(BF16)
        k_part = _dot(mb, wk_ref[...].astype(BF16))
        v_part = _dot(mb, wv_ref[...].astype(BF16))

        @pl.when(step == 0)
        def _():
            kacc_ref[...] = k_part
            vacc_ref[...] = v_part

        @pl.when(step > 0)
        def _():
            kacc_ref[...] += k_part
            vacc_ref[...] += v_part

        @pl.when(step == CA_WCHUNKS - 1)
        def _():
            for s in range(kt_s.shape[0]):
                rows = slice(s * N_MEM, (s + 1) * N_MEM)
                kt_s[s] = kacc_ref[rows].T.astype(BF16)
                v_s[s] = vacc_ref[rows].astype(BF16)

    head_slices = [slice(hd * CA_HEAD_DIM, (hd + 1) * CA_HEAD_DIM) for hd in range(CA_HEADS)]

    def q_proj(rows):
        return _dot(h_ref[rows].astype(BF16), wq_s[...]).astype(BF16)

    @pl.when(step >= CA_WCHUNKS)
    def _():
        seq = (step - CA_WCHUNKS) // steps_per_seq

        def score(q):
            return [_dot(q[:, sl], kt_s[seq, sl, :]) for sl in head_slices]

        blocks = _sub_blocks(CA_TM)
        scores = score(q_proj(blocks[0]))
        q_next = q_proj(blocks[1]) if len(blocks) > 1 else None
        for i, rows in enumerate(blocks):
            heads = []
            for s, sl in zip(scores, head_slices):
                p = jnp.exp(s - jnp.max(s, axis=-1, keepdims=True))
                p = p * (1.0 / jnp.sum(p, axis=-1, keepdims=True))
                heads.append(_dot(p.astype(BF16), v_s[seq, :, sl]))
            if i + 1 < len(blocks):
                scores = score(q_next)
            if i + 2 < len(blocks):
                q_next = q_proj(blocks[i + 2])
            o = jnp.concatenate(heads, axis=-1).astype(BF16)
            y = _dot(o, wo_s[...])
            o_ref[rows] = _layer_norm(ALPHA * h_ref[rows] + y, ln_g, ln_b)


def _cross_attn(h, batch, layer, mem, wk, wv, wq, wo, g, b):
    t = h.shape[0]
    m = mem.shape[0]
    mem_spec = pl.BlockSpec((m, D_MODEL // CA_WCHUNKS),
                            lambda i: (0, jnp.minimum(i, CA_WCHUNKS - 1)))
    staged = lambda w: _staged(w.shape, layer, CA_WCHUNKS)
    return pl.pallas_call(
        functools.partial(_ca_kernel, layer=layer, steps_per_seq=t // batch // CA_TM),
        grid=(CA_WCHUNKS + t // CA_TM,),
        in_specs=[_token_rows(CA_TM, CA_WCHUNKS), mem_spec, staged(wk), staged(wv), staged(wq),
                  staged(wo), _resident(g.shape), _resident(b.shape)],
        out_specs=_token_rows(CA_TM, CA_WCHUNKS),
        out_shape=jax.ShapeDtypeStruct((t, D_MODEL), F32),
        scratch_shapes=[
            pltpu.VMEM(wq.shape[1:], BF16),
            pltpu.VMEM(wo.shape[1:], BF16),
            pltpu.VMEM((m, D_MODEL), F32),
            pltpu.VMEM((m, D_MODEL), F32),
            pltpu.VMEM((batch, D_MODEL, N_MEM), BF16),
            pltpu.VMEM((batch, N_MEM, D_MODEL), BF16),
        ],
        compiler_params=pltpu.CompilerParams(
            dimension_semantics=("arbitrary",), vmem_limit_bytes=VMEM_LIMIT_BYTES),
        name="cross_attn_ln",
    )(h, mem, wk, wv, wq, wo, g, b)


def _pool_block_diag(w):
    nl, ng, c, d = w.shape
    eye = jnp.eye(ng, dtype=w.dtype)
    return (w[:, :, :, None, :] * eye[None, :, None, :, None]).reshape(
        nl, ng * c, ng * d).astype(BF16)


def kernel(x, mem, ffn1_gate, ffn1_up, ffn1_down, w_in, conv_w, rec_lb, rec_norm_g, pool_w,
           pool_scale, w_out, ca_q, ca_k, ca_v, ca_o, ffn2_gate, ffn2_up, ffn2_down, ln_g, ln_b):
    batch, seq, _ = x.shape
    h = x.reshape(batch * seq, D_MODEL)
    mem2 = mem.reshape(batch * N_MEM, D_MODEL)
    pool_bd = _pool_block_diag(pool_w)
    for l in range(DEPTH):
        h = _ffn(h, l, 0, ffn1_gate, ffn1_up, ffn1_down, ln_g, ln_b)
        h = _mixer(h, batch, l, w_in, conv_w, rec_lb, rec_norm_g, pool_bd, pool_scale, w_out,
                   ln_g, ln_b)
        h = _cross_attn(h, batch, l, mem2, ca_k, ca_v, ca_q, ca_o, ln_g, ln_b)
        h = _ffn(h, l, 3, ffn2_gate, ffn2_up, ffn2_down, ln_g, ln_b)
    return h.reshape(batch, seq, D_MODEL)
```

```python
import functools

import jax
import jax.numpy as jnp
from jax import lax
from jax.experimental import pallas as pl
from jax.experimental.pallas import tpu as pltpu

D_MODEL = 1024
DEPTH = 2
N_MEM = 256
CONV_DIM = 256
CONV_WIDTH = 3
REC_HEADS = 4
REC_DIM = 512
REC_HEAD_DIM = 128
REC_FDIM = 512
POOL_DIM = 256
POOL_WINDOWS = (2, 4, 8, 16)
POOL_GROUP = 64
D_IN = 3 * CONV_DIM + 2 * REC_FDIM + 2 * REC_DIM + POOL_DIM
CA_HEADS = 4
CA_HEAD_DIM = 256
D_FF = 2816
ALPHA = (2.0 * DEPTH) ** 0.25
LN_EPS = 1e-5
RMS_EPS = 1e-6

F32 = jnp.float32
BF16 = jnp.bfloat16

VMEM_LIMIT_BYTES = 56 * 1024 * 1024
SUBLANES = 8
TM = 1024
CA_TM = 512
SUB = 256
FFN_WCHUNKS = 4
MIX_WCHUNKS = 4
CA_WCHUNKS = 2
TS = 256
HALF = TS // 2
PAIR = 2 * REC_HEAD_DIM
CONV_HIST = 8
POOL_HIST = 16
LEVELS = (1, 2, 4, 8, 16, 32, 64, 128)
FILL_COLS = 256
_PAIR_PLAN_FIRST = (1, 1, 1, 0, 0, 0, 0, 0) + (0, 0) + (1, 1)
_PAIR_PLAN_LATER = (1, 1, 1, 1, 0, 0, 0, 0) + (0, 0) + (1, 1)
FILL_PLAN_FIRST = (2, 1) + _PAIR_PLAN_FIRST + (1, 1, 0, 0, 0, 0, 0, 0) + (0, 0) + (1, 1)
FILL_PLAN_LATER = (0, 0) + _PAIR_PLAN_LATER + _PAIR_PLAN_LATER
MIX_BLOCKS = 2

assert HALF == REC_HEAD_DIM and D_IN % FILL_COLS == 0


def _layer_norm(y, g, b):
    mu = jnp.mean(y, axis=-1, keepdims=True)
    d = y - mu
    var = jnp.mean(d * d, axis=-1, keepdims=True)
    return d * lax.rsqrt(var + LN_EPS) * g + b


def _dot(a, b):
    return jnp.dot(a, b, preferred_element_type=F32)


def _dot_nt(a, b):
    return lax.dot_general(a, b, (((1,), (1,)), ((), ())), preferred_element_type=F32)


def _resident(shape):
    return pl.BlockSpec(shape, lambda *_: (0,) * len(shape), pipeline_mode=pl.Buffered(1))


def _staged(stacked_shape, layer, nchunks):
    _, rows, cols = stacked_shape
    return pl.BlockSpec((None, rows // nchunks, cols),
                        lambda i: (layer, jnp.minimum(i, nchunks - 1), 0))


def _ln_params(g_ref, b_ref, layer, idx):
    return g_ref[layer, idx:idx + 1, :], b_ref[layer, idx:idx + 1, :]


def _token_rows(rows, nchunks):
    return pl.BlockSpec((rows, D_MODEL), lambda i: (jnp.maximum(i - nchunks, 0), 0))


def _sub_blocks(step_rows=TM):
    return [slice(lo, lo + SUB) for lo in range(0, step_rows, SUB)]


def _stage_chunk(i, src_ref, dst_ref, scale=None):
    rows = src_ref.shape[0]
    w = src_ref[...] if scale is None else src_ref[...] * scale
    dst_ref[pl.ds(pl.multiple_of(i * rows, rows), rows), :] = w.astype(BF16)


def _ffn_kernel(h_ref, wg_ref, wu_ref, wd_ref, g_ref, b_ref, o_ref, wg_s, wu_s, wd_s,
                *, layer, ln_idx):
    step = pl.program_id(0)
    ln_g, ln_b = _ln_params(g_ref, b_ref, layer, ln_idx)

    @pl.when(step < FFN_WCHUNKS)
    def _():
        _stage_chunk(step, wg_ref, wg_s)
        _stage_chunk(step, wu_ref, wu_s)
        _stage_chunk(step, wd_ref, wd_s, scale=0.5)

    def up_stage(rows):
        xb = h_ref[rows].astype(BF16)
        gate = _dot(xb, wg_s[...])
        up = _dot(xb, wu_s[...])
        return (gate * jax.nn.sigmoid(gate) * up).astype(BF16)

    def down_stage(rows, act):
        y = _dot(act, wd_s[...])
        o_ref[rows] = _layer_norm(ALPHA * h_ref[rows] + y, ln_g, ln_b)

    @pl.when(step >= FFN_WCHUNKS)
    def _():
        blocks = _sub_blocks()
        act = up_stage(blocks[0])
        for prev, rows in zip(blocks, blocks[1:]):
            nxt = up_stage(rows)
            down_stage(prev, act)
            act = nxt
        down_stage(blocks[-1], act)


def _ffn(h, layer, ln_idx, wg, wu, wd, g, b):
    t = h.shape[0]
    return pl.pallas_call(
        functools.partial(_ffn_kernel, layer=layer, ln_idx=ln_idx),
        grid=(FFN_WCHUNKS + t // TM,),
        in_specs=[_token_rows(TM, FFN_WCHUNKS)]
                 + [_staged(w.shape, layer, FFN_WCHUNKS) for w in (wg, wu, wd)]
                 + [_resident(g.shape), _resident(b.shape)],
        out_specs=_token_rows(TM, FFN_WCHUNKS),
        out_shape=jax.ShapeDtypeStruct((t, D_MODEL), F32),
        scratch_shapes=[pltpu.VMEM(wg.shape[1:], BF16), pltpu.VMEM(wu.shape[1:], BF16),
                        pltpu.VMEM(wd.shape[1:], BF16)],
        compiler_params=pltpu.CompilerParams(
            dimension_semantics=("arbitrary",), vmem_limit_bytes=VMEM_LIMIT_BYTES),
        name="ffn_ln",
    )(h, wg, wu, wd, g, b)


def _mixer_tile(x, z, tile_idx, layer, fill, convw_ref, reclb_ref, normg_ref, poolw_ref,
                poolscale_ref, wout_ref, g_ref, beta_ref, state_ref, uprev_ref, pprev_ref, bcum_ref):
    def proj(lo, width):
        return z[:, lo:lo + width]

    fill()

    c0 = 0
    cb = proj(c0, CONV_DIM)
    cc = proj(c0 + CONV_DIM, CONV_DIM)
    ch = proj(c0 + 2 * CONV_DIM, CONV_DIM)
    u = cc * ch
    uext = jnp.concatenate([uprev_ref[...], u], axis=0)
    u1 = pltpu.roll(uext, 1, axis=0)[CONV_HIST:]
    u2 = pltpu.roll(uext, 2, axis=0)[CONV_HIST:]
    uprev_ref[...] = u[TS - CONV_HIST:]
    cw = convw_ref[layer]
    y_conv = cb * (cw[2:3] * u + cw[1:2] * u1 + cw[0:1] * u2)

    p0 = 3 * CONV_DIM + 2 * REC_FDIM + 2 * REC_DIM
    pu = proj(p0, POOL_DIM)
    pext = jnp.concatenate([pprev_ref[...], pu], axis=0)
    pprev_ref[...] = pu[TS - POOL_HIST:]
    lane = lax.broadcasted_iota(jnp.int32, (TS, POOL_DIM), 1)
    tpos = tile_idx * TS + lax.broadcasted_iota(jnp.int32, (TS, POOL_DIM), 0)
    acc = pext
    wsum = None
    win = None
    span = 1
    for gi, w in enumerate(POOL_WINDOWS):
        while span < w:
            acc = acc + pltpu.roll(acc, span, axis=0)
            span *= 2
        part = acc[POOL_HIST:]
        in_group = lane >= gi * POOL_GROUP
        wsum = part if wsum is None else jnp.where(in_group, part, wsum)
        win = jnp.full((TS, POOL_DIM), w, jnp.int32) if win is None else jnp.where(in_group, w, win)
    count = jnp.minimum(tpos + 1, win).astype(F32)
    pooled = (wsum / count - pu).astype(BF16)
    y_pool = _dot(pooled, poolw_ref[layer]) * poolscale_ref[layer:layer + 1, :]
    fill()

    r0 = 3 * CONV_DIM
    rl = reclb_ref[...]
    e = jnp.exp(rl - jnp.max(rl, axis=0, keepdims=True))
    sm = e / jnp.sum(e, axis=0, keepdims=True)
    lb_all = jnp.sum(sm[0:layer + 1], axis=0, keepdims=True) - sm[0:1]

    row2 = lax.broadcasted_iota(jnp.int32, (TS, TS), 0)
    col2 = lax.broadcasted_iota(jnp.int32, (TS, TS), 1)
    tri = jnp.where(row2 >= col2, 1.0, 0.0).astype(BF16)
    rbit = lax.broadcasted_iota(jnp.int32, (TS, PAIR), 0)
    sub3 = lax.broadcasted_iota(jnp.int32, (TS // SUBLANES, SUBLANES, PAIR), 1)
    trow = lax.broadcasted_iota(jnp.int32, (HALF, PAIR), 0)
    tcol = lax.broadcasted_iota(jnp.int32, (HALF, PAIR), 1) & (HALF - 1)
    xor = trow ^ tcol
    causal = trow > tcol
    lvl_mask = {half: xor >= half for half in LEVELS[:-1]}
    zero_pair = jnp.zeros((HALF, PAIR), F32)
    zero_blk = jnp.zeros((HALF, HALF), F32)
    zero_key = jnp.zeros((HALF, REC_HEAD_DIM), BF16)

    def pair_nt(lhs, keys):
        rhs = jnp.concatenate(
            [jnp.concatenate([keys[:, :REC_HEAD_DIM], zero_key], axis=1),
             jnp.concatenate([zero_key, keys[:, REC_HEAD_DIM:]], axis=1)], axis=0)
        return _dot_nt(lhs, rhs)

    y_rec = []
    for pr in range(REC_HEADS // 2):
        lanes = slice(pr * PAIR, (pr + 1) * PAIR)
        rq = proj(r0 + pr * PAIR, PAIR)
        rf = proj(r0 + REC_FDIM + pr * PAIR, PAIR)
        rv = proj(r0 + 2 * REC_FDIM + pr * PAIR, PAIR)
        rg = proj(r0 + 2 * REC_FDIM + REC_DIM + pr * PAIR, PAIR)
        lb = lb_all[:, lanes]

        f = lb + (1.0 - lb) * jax.nn.sigmoid(rf)
        log2f = jnp.log2(f)
        k = 1.0 - f

        l_hi = log2f.astype(BF16)
        l_lo = (log2f - l_hi.astype(F32)).astype(BF16)
        bcum = _dot(tri, l_hi) + _dot(tri, l_lo)
        bcum_ref[:, lanes] = bcum
        b_last = bcum_ref[TS - 1:TS, lanes]

        q_in = (rq * jnp.exp2(bcum)).astype(BF16)
        k_out = (k * jnp.exp2(b_last - bcum)).astype(BF16)
        decay_all = jnp.exp2(b_last)

        b3 = bcum.reshape(TS // SUBLANES, SUBLANES, PAIR)

        def level_operand(half):
            if half >= SUBLANES:
                parts = []
                for gidx in range(TS // (2 * half)):
                    lo = 2 * half * gidx
                    ref = bcum_ref[lo + half - 1:lo + half, lanes]
                    parts.append(k[lo:lo + half] * jnp.exp2(ref - bcum[lo:lo + half]))
                    parts.append(rq[lo + half:lo + 2 * half]
                                 * jnp.exp2(bcum[lo + half:lo + 2 * half] - ref))
                return jnp.concatenate(parts, axis=0)
            if half == 1:
                ref = jnp.where((sub3 & 1) == 1, pltpu.roll(b3, 1, axis=1), b3)
            elif half == 2:
                m = sub3 & 3
                ref = jnp.where(m == 0, pltpu.roll(b3, SUBLANES - 1, axis=1),
                                jnp.where(m == 1, b3,
                                          jnp.where(m == 2, pltpu.roll(b3, 1, axis=1),
                                                    pltpu.roll(b3, 2, axis=1))))
            else:
                ref = jnp.concatenate(
                    [jnp.broadcast_to(
                        bcum_ref[SUBLANES * gidx + half - 1:SUBLANES * gidx + half, lanes],
                        (1, SUBLANES, PAIR))
                     for gidx in range(TS // SUBLANES)], axis=0)
            gfac = jnp.exp2(-jnp.abs(b3 - ref)).reshape(TS, PAIR)
            return jnp.where((rbit & half) != 0, rq, k) * gfac

        w_lvl = {}
        for half in LEVELS:
            fill()
            w_lvl[half] = level_operand(half).astype(BF16)

        diag = []
        for d in range(2):
            rows = slice(d * HALF, (d + 1) * HALF)
            blk = zero_pair
            for half in LEVELS[:-1]:
                wl = w_lvl[half][rows]
                blk = jnp.where(lvl_mask[half], pair_nt(wl, wl), blk)
            diag.append(jnp.where(causal, blk, 0.0))
            fill()
        wl = w_lvl[HALF]
        cross = pair_nt(wl[HALF:], wl[:HALF])

        sg = jax.nn.sigmoid(rg)
        ng = normg_ref[layer:layer + 1, lanes]
        for j in range(2):
            fill()
            sl = slice(j * REC_HEAD_DIM, (j + 1) * REC_HEAD_DIM)
            attn = jnp.concatenate(
                [jnp.concatenate([diag[0][:, sl], zero_blk], axis=1),
                 jnp.concatenate([cross[:, sl], diag[1][:, sl]], axis=1)], axis=0).astype(BF16)
            qh = rq[:, sl]
            kh = k[:, sl]
            vh = rv[:, sl]
            st = state_ref[2 * pr + j]
            o = (_dot(attn, vh.astype(BF16))
                 + jnp.sum(qh * kh, axis=-1, keepdims=True) * vh
                 + _dot_nt(q_in[:, sl], st.astype(BF16)))
            state_ref[2 * pr + j] = (st * decay_all[:, sl]
                                     + _dot(vh.T.astype(BF16), k_out[:, sl]))
            o = o * lax.rsqrt(jnp.mean(o * o, axis=-1, keepdims=True) + RMS_EPS)
            y_rec.append(o * ng[:, sl] * sg[:, sl])

    mix = jnp.concatenate([y_conv] + y_rec + [y_pool], axis=-1).astype(BF16)
    y = _dot(mix, wout_ref[...])
    return _layer_norm(ALPHA * x + y, *_ln_params(g_ref, beta_ref, layer, 1))


def _mixer_kernel(h_ref, hnext_ref, win_ref, convw_ref, reclb_ref, normg_ref, poolw_ref,
                  poolscale_ref, wout_ref, g_ref, beta_ref, o_ref,
                  win_s, wout_s, z0_ref, z1_ref, state_ref, uprev_ref, pprev_ref, bcum_ref,
                  *, layer, steps_per_seq):
    step = pl.program_id(0)

    @pl.when(step < MIX_WCHUNKS)
    def _():
        _stage_chunk(step, win_ref, win_s)
        _stage_chunk(step, wout_ref, wout_s)

    @pl.when(step >= MIX_WCHUNKS)
    def _():
        s_idx = lax.rem(step - MIX_WCHUNKS, steps_per_seq)

        @pl.when(s_idx == 0)
        def _():
            state_ref[...] = jnp.zeros_like(state_ref)
            uprev_ref[...] = jnp.zeros_like(uprev_ref)
            pprev_ref[...] = jnp.zeros_like(pprev_ref)

        @pl.when(step == MIX_WCHUNKS)
        def _():
            z0_ref[...] = _dot(h_ref[0:TS].astype(BF16), win_s[...])

        tile_refs = (convw_ref, reclb_ref, normg_ref, poolw_ref, poolscale_ref, wout_s, g_ref,
                     beta_ref, state_ref, uprev_ref, pprev_ref, bcum_ref)

        def projector(xb, dst_ref, plan):
            todo = list(range(0, D_IN, FILL_COLS))
            counts = list(plan)

            def fill(flush=False):
                for _ in range(len(todo) if flush else counts.pop(0)):
                    lo = todo.pop(0)
                    dst_ref[:, lo:lo + FILL_COLS] = _dot(xb, win_s[:, lo:lo + FILL_COLS])

            return fill

        z_refs = (z0_ref, z1_ref)
        for j in range(MIX_BLOCKS):
            rows = slice(j * TS, (j + 1) * TS)
            if j + 1 < MIX_BLOCKS:
                upcoming = h_ref[(j + 1) * TS:(j + 2) * TS]
            else:
                upcoming = hnext_ref[...]
            fill = projector(upcoming.astype(BF16), z_refs[(j + 1) % 2],
                             FILL_PLAN_FIRST if j == 0 else FILL_PLAN_LATER)
            o_ref[rows] = _mixer_tile(h_ref[rows], z_refs[j % 2][...], MIX_BLOCKS * s_idx + j,
                                      layer, fill, *tile_refs)
            fill(flush=True)


def _mixer(h, batch, layer, w_in, conv_w, rec_lb, norm_g, pool_w, pool_scale, w_out, g, b):
    t = h.shape[0]
    steps_per_seq = t // batch // (MIX_BLOCKS * TS)
    last_tile = t // TS - 1
    nxt = pl.BlockSpec(
        (TS, D_MODEL),
        lambda i: (jnp.minimum(MIX_BLOCKS * (jnp.maximum(i - MIX_WCHUNKS, 0) + 1), last_tile), 0))
    small = (conv_w, rec_lb, norm_g, pool_w, pool_scale)
    return pl.pallas_call(
        functools.partial(_mixer_kernel, layer=layer, steps_per_seq=steps_per_seq),
        grid=(MIX_WCHUNKS + batch * steps_per_seq,),
        in_specs=([_token_rows(MIX_BLOCKS * TS, MIX_WCHUNKS), nxt,
                   _staged(w_in.shape, layer, MIX_WCHUNKS)]
                  + [_resident(p.shape) for p in small]
                  + [_staged(w_out.shape, layer, MIX_WCHUNKS), _resident(g.shape),
                     _resident(b.shape)]),
        out_specs=_token_rows(MIX_BLOCKS * TS, MIX_WCHUNKS),
        out_shape=jax.ShapeDtypeStruct((t, D_MODEL), F32),
        scratch_shapes=[
            pltpu.VMEM(w_in.shape[1:], BF16),
            pltpu.VMEM(w_out.shape[1:], BF16),
            pltpu.VMEM((TS, D_IN), F32),
            pltpu.VMEM((TS, D_IN), F32),
            pltpu.VMEM((REC_HEADS, REC_HEAD_DIM, REC_HEAD_DIM), F32),
            pltpu.VMEM((CONV_HIST, CONV_DIM), F32),
            pltpu.VMEM((POOL_HIST, POOL_DIM), F32),
            pltpu.VMEM((TS, REC_FDIM), F32),
        ],
        compiler_params=pltpu.CompilerParams(
            dimension_semantics=("arbitrary",), vmem_limit_bytes=VMEM_LIMIT_BYTES),
        name="mixer_ln",
    )(h, h, w_in, *small, w_out, g, b)


def _ca_kernel(h_ref, mem_ref, wk_ref, wv_ref, wq_ref, wo_ref, g_ref, b_ref, o_ref,
               wq_s, wo_s, kacc_ref, vacc_ref, qk_s, vo_s, *, layer, steps_per_seq):
    step = pl.program_id(0)
    ln_g, ln_b = _ln_params(g_ref, b_ref, layer, 2)
    head_slices = [slice(hd * CA_HEAD_DIM, (hd + 1) * CA_HEAD_DIM) for hd in range(CA_HEADS)]

    @pl.when(step < CA_WCHUNKS)
    def _():
        _stage_chunk(step, wq_ref, wq_s, scale=CA_HEAD_DIM ** -0.5)
        _stage_chunk(step, wo_ref, wo_s)

        mb = mem_ref[...].astype(BF16)
        k_part = _dot(mb, wk_ref[...].astype(BF16))
        v_part = _dot(mb, wv_ref[...].astype(BF16))

        @pl.when(step == 0)
        def _():
            kacc_ref[...] = k_part
            vacc_ref[...] = v_part

        @pl.when(step > 0)
        def _():
            kacc_ref[...] += k_part
            vacc_ref[...] += v_part

        @pl.when(step == CA_WCHUNKS - 1)
        def _():
            for s in range(qk_s.shape[0]):
                rows = slice(s * N_MEM, (s + 1) * N_MEM)
                kb = kacc_ref[rows].astype(BF16)
                vb = vacc_ref[rows].astype(BF16)
                for hd, sl in enumerate(head_slices):
                    cols = slice(hd * N_MEM, (hd + 1) * N_MEM)
                    qk_s[s, :, cols] = _dot_nt(wq_s[:, sl], kb[:, sl]).astype(BF16)
                    vo_s[s, cols, :] = _dot(vb[:, sl], wo_s[sl, :]).astype(BF16)

    @pl.when(step >= CA_WCHUNKS)
    def _():
        seq = (step - CA_WCHUNKS) // steps_per_seq
        mem_slices = [slice(hd * N_MEM, (hd + 1) * N_MEM) for hd in range(CA_HEADS)]

        def score(rows):
            return _dot(h_ref[rows].astype(BF16), qk_s[seq])

        blocks = _sub_blocks(CA_TM)
        scores = score(blocks[0])
        for i, rows in enumerate(blocks):
            nxt = score(blocks[i + 1]) if i + 1 < len(blocks) else None
            probs = []
            for sl in mem_slices:
                s = scores[:, sl]
                p = jnp.exp(s - jnp.max(s, axis=-1, keepdims=True))
                probs.append((p * (1.0 / jnp.sum(p, axis=-1, keepdims=True))).astype(BF16))
            y = _dot(jnp.concatenate(probs, axis=-1), vo_s[seq])
            o_ref[rows] = _layer_norm(ALPHA * h_ref[rows] + y, ln_g, ln_b)
            scores = nxt


def _cross_attn(h, batch, layer, mem, wk, wv, wq, wo, g, b):
    t = h.shape[0]
    m = mem.shape[0]
    mem_spec = pl.BlockSpec((m, D_MODEL // CA_WCHUNKS),
                            lambda i: (0, jnp.minimum(i, CA_WCHUNKS - 1)))
    staged = lambda w: _staged(w.shape, layer, CA_WCHUNKS)
    return pl.pallas_call(
        functools.partial(_ca_kernel, layer=layer, steps_per_seq=t // batch // CA_TM),
        grid=(CA_WCHUNKS + t // CA_TM,),
        in_specs=[_token_rows(CA_TM, CA_WCHUNKS), mem_spec, staged(wk), staged(wv), staged(wq),
                  staged(wo), _resident(g.shape), _resident(b.shape)],
        out_specs=_token_rows(CA_TM, CA_WCHUNKS),
        out_shape=jax.ShapeDtypeStruct((t, D_MODEL), F32),
        scratch_shapes=[
            pltpu.VMEM(wq.shape[1:], BF16),
            pltpu.VMEM(wo.shape[1:], BF16),
            pltpu.VMEM((m, D_MODEL), F32),
            pltpu.VMEM((m, D_MODEL), F32),
            pltpu.VMEM((batch, D_MODEL, CA_HEADS * N_MEM), BF16),
            pltpu.VMEM((batch, CA_HEADS * N_MEM, D_MODEL), BF16),
        ],
        compiler_params=pltpu.CompilerParams(
            dimension_semantics=("arbitrary",), vmem_limit_bytes=VMEM_LIMIT_BYTES),
        name="cross_attn_ln",
    )(h, mem, wk, wv, wq, wo, g, b)


def _pool_block_diag(w):
    nl, ng, c, d = w.shape
    eye = jnp.eye(ng, dtype=w.dtype)
    return (w[:, :, :, None, :] * eye[None, :, None, :, None]).reshape(
        nl, ng * c, ng * d).astype(BF16)


def kernel(x, mem, ffn1_gate, ffn1_up, ffn1_down, w_in, conv_w, rec_lb, rec_norm_g, pool_w,
           pool_scale, w_out, ca_q, ca_k, ca_v, ca_o, ffn2_gate, ffn2_up, ffn2_down, ln_g, ln_b):
    batch, seq, _ = x.shape
    h = x.reshape(batch * seq, D_MODEL)
    mem2 = mem.reshape(batch * N_MEM, D_MODEL)
    pool_bd = _pool_block_diag(pool_w)
    for l in range(DEPTH):
        h = _ffn(h, l, 0, ffn1_gate, ffn1_up, ffn1_down, ln_g, ln_b)
        h = _mixer(h, batch, l, w_in, conv_w, rec_lb, rec_norm_g, pool_bd, pool_scale, w_out,
                   ln_g, ln_b)
        h = _cross_attn(h, batch, l, mem2, ca_k, ca_v, ca_q, ca_o, ln_g, ln_b)
        h = _ffn(h, l, 3, ffn2_gate, ffn2_up, ffn2_down, ln_g, ln_b)
    return h.reshape(batch, seq, D_MODEL)
```
